```python
import jax, jax.numpy as jnp
from jax import lax
import numpy as np

D_MODEL = 1024
BATCH = 4
SEQ = 8192
DEPTH = 4
DEC_BATCH = 32
DEC_SEQ = 64
PAST_LEN = 1024

CHUNK = 64
D_PLE = 256
D_CONV = D_MODEL // 2
CONV_WIDTH = 31
SB_HEADS = 4
SB_HEAD_DIM = 128
SB_WIDTH = SB_HEADS * SB_HEAD_DIM
SB_SCALE = SB_HEAD_DIM ** -0.5
D_FF = 2816
Q_BLOCK = 128
KEY_BLOCK = 128
EPS = 1e-6
D_IN = 2 * D_CONV + 3 * SB_WIDTH + 2 * D_MODEL

kernel_name = 'gated_conformer_stickbreaking_stream_step'


def rms_norm(x, g):
    xf = x.astype(jnp.float32)
    y = xf * lax.rsqrt(jnp.mean(xf * xf, axis=-1, keepdims=True) + EPS)
    return (y * g.astype(jnp.float32)).astype(x.dtype)


def layer_norm(x, g, b):
    xf = x.astype(jnp.float32)
    mu = jnp.mean(xf, axis=-1, keepdims=True)
    xc = xf - mu
    y = xc * lax.rsqrt(jnp.mean(xc * xc, axis=-1, keepdims=True) + EPS)
    return (y * g.astype(jnp.float32) + b.astype(jnp.float32)).astype(x.dtype)


def swiglu(x, w_in, w_down):
    gate, up = jnp.split(x @ w_in, 2, axis=-1)
    return (jax.nn.silu(gate) * up) @ w_down


def _sb_block(q, k, v, q_pos, k_pos):
    pad = (-k.shape[1]) % KEY_BLOCK
    if pad:
        k = jnp.pad(k, ((0, 0), (0, pad), (0, 0), (0, 0)))
        v = jnp.pad(v, ((0, 0), (0, pad), (0, 0), (0, 0)))
        k_pos = jnp.concatenate([k_pos, jnp.full((pad,), jnp.iinfo(jnp.int32).max, jnp.int32)])
    B, Tq, H, _ = q.shape
    nkb = k.shape[1] // KEY_BLOCK
    z = jnp.einsum('bqhd,bkhd->bhqk', q, k, preferred_element_type=jnp.float32) * SB_SCALE
    mask = k_pos[None, :] < q_pos[:, None]
    lk = jnp.where(mask, jax.nn.log_sigmoid(-z), 0.0).reshape(B, H, Tq, nkb, KEY_BLOCK)
    tot = jnp.sum(lk, axis=-1)
    after = lax.cumsum(tot, axis=3, reverse=True) - tot
    suffix = (lax.cumsum(lk, axis=4, reverse=True) + after[..., None]).reshape(B, H, Tq, -1)
    w = jnp.exp(jnp.where(mask, z + suffix, -jnp.inf))
    return jnp.einsum('bhqk,bkhd->bqhd', w.astype(v.dtype), v)


def sb_attention_prompt(q, k, v):
    T = q.shape[1]
    pos = jnp.arange(T, dtype=jnp.int32)
    outs = []
    for j in range(T // Q_BLOCK):
        s0, e = j * Q_BLOCK, (j + 1) * Q_BLOCK
        outs.append(_sb_block(q[:, s0:e], k[:, :e], v[:, :e], pos[s0:e], pos[:e]))
    return jnp.concatenate(outs, axis=1)


def conv_module(a, hist, lp):
    xa = jnp.concatenate([hist, a], axis=1)
    y = lax.conv_general_dilated(xa, lp['conv_w'][:, None, :], window_strides=(1,), padding='VALID',
                                 dimension_numbers=('NWC', 'WIO', 'NWC'), feature_group_count=D_CONV)
    y = y + lp['conv_b']
    y = jax.nn.silu(layer_norm(y, lp['conv_ln_g'], lp['conv_ln_b']))
    return y @ lp['w_conv_out'], xa[:, -(CONV_WIDTH - 1):]


def trunk_layer(x, p, conv_hist, k_hist, v_hist, lp):
    B, T, _ = x.shape
    h = swiglu(rms_norm(x, lp['ffn1_pre']), lp['ffn1_in'], lp['ffn1_down'])
    x = x + 0.5 * rms_norm(h, lp['ffn1_post'])
    u = rms_norm(x, lp['mix_pre'])
    c0 = 2 * D_CONV
    glu_in, q, k, v, gate_logits = jnp.split(
        u @ lp['w_in'], [c0, c0 + SB_WIDTH, c0 + 2 * SB_WIDTH, c0 + 3 * SB_WIDTH], axis=-1)
    a = glu_in[..., :D_CONV] * jax.nn.sigmoid(glu_in[..., D_CONV:])
    conv_out, conv_state = conv_module(a, conv_hist, lp)
    q = q.reshape(B, T, SB_HEADS, SB_HEAD_DIM)
    k = k.reshape(B, T, SB_HEADS, SB_HEAD_DIM)
    v = v.reshape(B, T, SB_HEADS, SB_HEAD_DIM)
    if k_hist is None:
        att = sb_attention_prompt(q, k, v)
    else:
        past = k_hist.shape[1]
        k_all = jnp.concatenate([k_hist, k], axis=1)
        v_all = jnp.concatenate([v_hist, v], axis=1)
        att = _sb_block(q, k_all, v_all, past + jnp.arange(T, dtype=jnp.int32),
                        jnp.arange(past + T, dtype=jnp.int32))
    att = att.reshape(B, T, SB_WIDTH) @ lp['w_attn_out']
    g_conv, g_att = jnp.split(jax.nn.sigmoid(gate_logits), 2, axis=-1)
    mix = (g_conv * conv_out + g_att * att) @ lp['w_out']
    x = x + rms_norm(mix, lp['mix_post'])
    h = swiglu(rms_norm(x, lp['ffn2_pre']), lp['ffn2_in'], lp['ffn2_down'])
    x = x + 0.5 * rms_norm(h, lp['ffn2_post'])
    gate = jax.nn.sigmoid(rms_norm(x, lp['ple_pre']) @ lp['w_ple_gate'])
    x = x + rms_norm((p @ lp['w_ple']) * gate, lp['ple_post'])
    return x, k, v, conv_state


def _normal(key, shape, scale):
    return jax.random.normal(key, shape, jnp.float32) * scale


def setup_inputs(seed: int = 0) -> dict:
    key = jax.random.key(seed)
    ks = iter(jax.random.split(key, 40))
    L, D, F = DEPTH, D_MODEL, D_FF
    gain = lambda shape: 1.0 + _normal(next(ks), shape, 0.02)
    return {
        'x_prompt': _normal(next(ks), (BATCH, SEQ, D), 1.0),
        'x_sample': _normal(next(ks), (DEC_BATCH, DEC_SEQ, D), 1.0),
        'p_prompt': _normal(next(ks), (L, BATCH, SEQ, D_PLE), 1.0),
        'p_sample': _normal(next(ks), (L, DEC_BATCH, DEC_SEQ, D_PLE), 1.0),
        'cache_k': _normal(next(ks), (L, DEC_BATCH, PAST_LEN, SB_HEADS, SB_HEAD_DIM), 1.0),
        'cache_v': _normal(next(ks), (L, DEC_BATCH, PAST_LEN, SB_HEADS, SB_HEAD_DIM), 1.0),
        'state_conv': _normal(next(ks), (L, DEC_BATCH, CONV_WIDTH - 1, D_CONV), 0.5),
        'ffn1_pre': gain((L, D)),
        'ffn1_post': gain((L, D)),
        'ffn1_in': _normal(next(ks), (L, D, 2 * F), D ** -0.5),
        'ffn1_down': _normal(next(ks), (L, F, D), F ** -0.5),
        'mix_pre': gain((L, D)),
        'mix_post': gain((L, D)),
        'w_in': _normal(next(ks), (L, D, D_IN), D ** -0.5),
        'conv_w': _normal(next(ks), (L, CONV_WIDTH, D_CONV), CONV_WIDTH ** -0.5),
        'conv_b': _normal(next(ks), (L, D_CONV), 0.02),
        'conv_ln_g': gain((L, D_CONV)),
        'conv_ln_b': _normal(next(ks), (L, D_CONV), 0.02),
        'w_conv_out': _normal(next(ks), (L, D_CONV, D), D_CONV ** -0.5),
        'w_attn_out': _normal(next(ks), (L, SB_WIDTH, D), SB_WIDTH ** -0.5),
        'w_out': _normal(next(ks), (L, D, D), D ** -0.5),
        'ffn2_pre': gain((L, D)),
        'ffn2_post': gain((L, D)),
        'ffn2_in': _normal(next(ks), (L, D, 2 * F), D ** -0.5),
        'ffn2_down': _normal(next(ks), (L, F, D), F ** -0.5),
        'ple_pre': gain((L, D)),
        'ple_post': gain((L, D)),
        'w_ple': _normal(next(ks), (L, D_PLE, D), D_PLE ** -0.5),
        'w_ple_gate': _normal(next(ks), (L, D, D), D ** -0.5),
        'final_norm': gain((D,)),
    }


def reference(x_prompt, x_sample, p_prompt, p_sample, cache_k, cache_v, state_conv,
              ffn1_pre, ffn1_post, ffn1_in, ffn1_down, mix_pre, mix_post, w_in,
              conv_w, conv_b, conv_ln_g, conv_ln_b, w_conv_out, w_attn_out, w_out,
              ffn2_pre, ffn2_post, ffn2_in, ffn2_down, ple_pre, ple_post, w_ple,
              w_ple_gate, final_norm):
    xp, xs = x_prompt, x_sample
    kp_l, vp_l, cp_l, ks_l, vs_l, cs_l = [], [], [], [], [], []
    for i in range(DEPTH):
        lp = {
            'ffn1_pre': ffn1_pre[i], 'ffn1_post': ffn1_post[i], 'ffn1_in': ffn1_in[i],
            'ffn1_down': ffn1_down[i], 'mix_pre': mix_pre[i], 'mix_post': mix_post[i],
            'w_in': w_in[i], 'conv_w': conv_w[i], 'conv_b': conv_b[i],
            'conv_ln_g': conv_ln_g[i], 'conv_ln_b': conv_ln_b[i],
            'w_conv_out': w_conv_out[i], 'w_attn_out': w_attn_out[i], 'w_out': w_out[i],
            'ffn2_pre': ffn2_pre[i], 'ffn2_post': ffn2_post[i], 'ffn2_in': ffn2_in[i],
            'ffn2_down': ffn2_down[i], 'ple_pre': ple_pre[i], 'ple_post': ple_post[i],
            'w_ple': w_ple[i], 'w_ple_gate': w_ple_gate[i],
        }
        zero_hist = jnp.zeros((xp.shape[0], CONV_WIDTH - 1, D_CONV), xp.dtype)
        xp, kp, vp, cp = trunk_layer(xp, p_prompt[i], zero_hist, None, None, lp)
        xs, ks_, vs_, cs_ = trunk_layer(xs, p_sample[i], state_conv[i], cache_k[i], cache_v[i], lp)
        kp_l.append(kp); vp_l.append(vp); cp_l.append(cp)
        ks_l.append(ks_); vs_l.append(vs_); cs_l.append(cs_)
    y_prompt = rms_norm(xp, final_norm)
    y_sample = rms_norm(xs, final_norm)
    return (y_prompt, y_sample,
            jnp.stack(kp_l), jnp.stack(vp_l), jnp.stack(cp_l),
            jnp.stack(ks_l), jnp.stack(vs_l), jnp.stack(cs_l))
```

```python
import functools

import jax
import jax.numpy as jnp
from jax import lax
from jax.experimental import pallas as pl
from jax.experimental.pallas import tpu as pltpu

D_MODEL = 1024
DEPTH = 4
D_PLE = 256
D_CONV = D_MODEL // 2
CONV_WIDTH = 31
CONV_HIST = CONV_WIDTH - 1
SB_HEADS = 4
SB_HEAD_DIM = 128
SB_WIDTH = SB_HEADS * SB_HEAD_DIM
SB_SCALE = SB_HEAD_DIM ** -0.5
D_FF = 2816
EPS = 1e-6
D_PROJ = 2 * D_CONV + 3 * SB_WIDTH
D_GATE = 2 * D_MODEL

KEY_BLOCK = 128
HALO = 32
CONV_ROWS = 64
TOKEN_TILE = 256
VMEM_LIMIT = 60 * 1024 * 1024

F32 = jnp.float32
BF16 = jnp.bfloat16


def _rms(x, g):
    return x * lax.rsqrt(jnp.mean(x * x, axis=-1, keepdims=True) + EPS) * g


def _dot(a, b):
    return jnp.dot(a, b, preferred_element_type=F32)


def _ffn(x, g_pre, g_post, w_in_ref, w_down_ref):
    u = _rms(x, g_pre).astype(BF16)
    h = _dot(u, w_in_ref[...])
    gate, up = h[:, :D_FF], h[:, D_FF:]
    act = (gate * jax.nn.sigmoid(gate) * up).astype(BF16)
    y = _dot(act, w_down_ref[...])
    return x + 0.5 * _rms(y, g_post)


def _pre_kernel(x_ref, g_pre_ref, g_post_ref, w1_ref, wd_ref, g_mix_ref, w_proj_ref,
                x1_ref, a_ref, q_ref, k_ref, v_ref):
    x1 = _ffn(x_ref[...], g_pre_ref[...], g_post_ref[...], w1_ref, wd_ref)
    x1_ref[...] = x1
    u = _rms(x1, g_mix_ref[...]).astype(BF16)
    proj = _dot(u, w_proj_ref[...])
    c = D_CONV
    a_ref[...] = proj[:, :c] * jax.nn.sigmoid(proj[:, c:2 * c])
    c = 2 * c
    q_ref[...] = (proj[:, c:c + SB_WIDTH] * SB_SCALE).astype(BF16)
    k_ref[...] = proj[:, c + SB_WIDTH:c + 2 * SB_WIDTH]
    v_ref[...] = proj[:, c + 2 * SB_WIDTH:c + 3 * SB_WIDTH]


def _resident(shape, layer):
    nd = len(shape)
    return pl.BlockSpec((None,) + tuple(shape[1:]), lambda i: (layer,) + (0,) * (nd - 1),
                        pipeline_mode=pl.Buffered(1))


def _rows(width, tile=TOKEN_TILE):
    return pl.BlockSpec((tile, width), lambda i: (i, 0))


def _pre_call(x, layer, w):
    n = x.shape[0]
    params = (w['ffn1_pre'], w['ffn1_post'], w['ffn1_in'], w['ffn1_down'], w['mix_pre'], w['w_proj'])
    return pl.pallas_call(
        _pre_kernel,
        grid=(n // TOKEN_TILE,),
        in_specs=[_rows(D_MODEL)] + [_resident(p.shape, layer) for p in params],
        out_specs=[_rows(D_MODEL), _rows(D_CONV), _rows(SB_WIDTH), _rows(SB_WIDTH), _rows(SB_WIDTH)],
        out_shape=[jax.ShapeDtypeStruct((n, D_MODEL), F32), jax.ShapeDtypeStruct((n, D_CONV), F32),
                   jax.ShapeDtypeStruct((n, SB_WIDTH), BF16), jax.ShapeDtypeStruct((n, SB_WIDTH), F32),
                   jax.ShapeDtypeStruct((n, SB_WIDTH), F32)],
        compiler_params=pltpu.CompilerParams(dimension_semantics=("parallel",),
                                             vmem_limit_bytes=VMEM_LIMIT),
        name="pre",
    )(x, *params)


def _sb_block(q, k_blk, v_blk, tri, carry, acc, mask):
    z = lax.dot_general(q, k_blk, (((1,), (1,)), ((), ())), preferred_element_type=F32)
    sp = jnp.maximum(z, 0.0) + jnp.log(1.0 + jnp.exp(-jnp.abs(z)))
    if mask is not None:
        sp = jnp.where(mask, sp, 0.0)
    hi = sp.astype(BF16)
    lo = (sp - hi.astype(F32)).astype(BF16)
    sums = _dot(jnp.concatenate([hi, lo], axis=1), tri)
    suffix, total = sums[:, :KEY_BLOCK], sums[:, KEY_BLOCK:]
    w = jnp.exp(z - suffix - carry)
    if mask is not None:
        w = jnp.where(mask, w, 0.0)
    acc = acc + _dot(w.astype(BF16), v_blk)
    return carry + total, acc


def _suffix_matrix():
    j = lax.broadcasted_iota(jnp.int32, (KEY_BLOCK, KEY_BLOCK), 0)
    s = lax.broadcasted_iota(jnp.int32, (KEY_BLOCK, KEY_BLOCK), 1)
    half = jnp.concatenate([(j >= s).astype(BF16), jnp.ones((KEY_BLOCK, KEY_BLOCK), BF16)], axis=1)
    return jnp.concatenate([half, half], axis=0)


def _attn_prompt_kernel(q_ref, k_ref, v_ref, tri_ref, o_ref, kb_ref, vb_ref):
    qi = pl.program_id(2)

    @pl.when(qi == 0)
    def _():
        kb_ref[...] = k_ref[...].astype(BF16)
        vb_ref[...] = v_ref[...].astype(BF16)

    q = q_ref[...]
    tri = tri_ref[...]
    t = lax.broadcasted_iota(jnp.int32, (KEY_BLOCK, KEY_BLOCK), 0)
    s = lax.broadcasted_iota(jnp.int32, (KEY_BLOCK, KEY_BLOCK), 1)
    zeros = jnp.zeros((KEY_BLOCK, KEY_BLOCK), F32)

    def keys(j):
        start = pl.multiple_of(j * KEY_BLOCK, KEY_BLOCK)
        return kb_ref[pl.ds(start, KEY_BLOCK), :], vb_ref[pl.ds(start, KEY_BLOCK), :]

    state = _sb_block(q, *keys(qi), tri, zeros, zeros, s < t)

    def body(i, state):
        return _sb_block(q, *keys(qi - 1 - i), tri, *state, None)

    _, acc = lax.fori_loop(0, qi, body, state)
    o_ref[...] = acc.astype(BF16)


def _attn_prompt_call(q, k, v, tri, batch, seq):
    n = q.shape[0]
    nq = seq // KEY_BLOCK
    q_spec = pl.BlockSpec((KEY_BLOCK, SB_HEAD_DIM), lambda b, h, i: (b * nq + i, h))
    kv_spec = pl.BlockSpec((seq, SB_HEAD_DIM), lambda b, h, i: (b, h))
    return pl.pallas_call(
        _attn_prompt_kernel,
        grid=(batch, SB_HEADS, nq),
        in_specs=[q_spec, kv_spec, kv_spec, pl.BlockSpec(tri.shape, lambda b, h, i: (0, 0))],
        out_specs=q_spec,
        out_shape=jax.ShapeDtypeStruct((n, SB_WIDTH), BF16),
        scratch_shapes=[pltpu.VMEM((seq, SB_HEAD_DIM), BF16), pltpu.VMEM((seq, SB_HEAD_DIM), BF16)],
        compiler_params=pltpu.CompilerParams(
            dimension_semantics=("parallel", "parallel", "arbitrary"), vmem_limit_bytes=VMEM_LIMIT),
        name="attn_prompt",
    )(q, k, v, tri)


def _attn_sample_kernel(q_ref, k_ref, v_ref, ck_ref, cv_ref, tri_ref, o_ref, kb_ref, vb_ref, *, past, t_new):
    pad = kb_ref.shape[0] - past - t_new
    for src, cache, dst in ((k_ref, ck_ref, kb_ref), (v_ref, cv_ref, vb_ref)):
        dst[0:past, :] = cache[...].astype(BF16)
        dst[past:past + t_new, :] = src[...].astype(BF16)
        dst[past + t_new:, :] = jnp.zeros((pad, SB_WIDTH), BF16)
    tri = tri_ref[...]
    n_blocks = kb_ref.shape[0] // KEY_BLOCK
    t = lax.broadcasted_iota(jnp.int32, (t_new, KEY_BLOCK), 0)
    s = lax.broadcasted_iota(jnp.int32, (t_new, KEY_BLOCK), 1)
    last_mask = s + ((n_blocks - 1) * KEY_BLOCK - past) < t
    for h in range(SB_HEADS):
        cols = slice(h * SB_HEAD_DIM, (h + 1) * SB_HEAD_DIM)
        q = q_ref[:, cols]
        state = (jnp.zeros((t_new, KEY_BLOCK), F32), jnp.zeros((t_new, SB_HEAD_DIM), F32))
        for j in range(n_blocks - 1, -1, -1):
            rows = slice(j * KEY_BLOCK, (j + 1) * KEY_BLOCK)
            mask = last_mask if j == n_blocks - 1 else None
            state = _sb_block(q, kb_ref[rows, cols], vb_ref[rows, cols], tri, *state, mask)
        o_ref[:, cols] = state[1].astype(BF16)


def _attn_sample_call(q, k, v, cache_k, cache_v, tri, layer, batch, t_new):
    n = q.shape[0]
    past = cache_k.shape[2]
    assert past % KEY_BLOCK == 0 and t_new <= KEY_BLOCK
    padded = past + KEY_BLOCK
    row_spec = pl.BlockSpec((t_new, SB_WIDTH), lambda b: (b, 0))
    cache_spec = pl.BlockSpec((None, None, past, SB_WIDTH), lambda b: (layer, b, 0, 0))
    return pl.pallas_call(
        functools.partial(_attn_sample_kernel, past=past, t_new=t_new),
        grid=(batch,),
        in_specs=[row_spec, row_spec, row_spec, cache_spec, cache_spec,
                  pl.BlockSpec(tri.shape, lambda b: (0, 0))],
        out_specs=row_spec,
        out_shape=jax.ShapeDtypeStruct((n, SB_WIDTH), BF16),
        scratch_shapes=[pltpu.VMEM((padded, SB_WIDTH), BF16), pltpu.VMEM((padded, SB_WIDTH), BF16)],
        compiler_params=pltpu.CompilerParams(dimension_semantics=("parallel",),
                                             vmem_limit_bytes=VMEM_LIMIT),
        name="attn_sample",
    )(q, k, v, cache_k, cache_v, tri)


def _post_kernel(x1_ref, a_ref, hist_ref, att_ref, p_ref,
                 g_mix_ref, conv_w_ref, conv_b_ref, ln_g_ref, ln_b_ref, w_conv_out_ref, w_attn_out_ref,
                 w_gate_ref, w_out_ref, g_mix_post_ref, g_pre_ref, g_post_ref, w2_ref, wd2_ref,
                 g_ple_pre_ref, g_ple_post_ref, w_ple_ref, w_ple_gate_ref, g_final_ref,
                 o_ref, xa_ref, y_ref, *, seq_tile, seq_len, prompt, final):
    nb = xa_ref.shape[0]
    if prompt:
        at_start = (pl.program_id(0) * seq_tile) % seq_len == 0
        xa_ref[0, 0:HALO, :] = jnp.where(at_start, 0.0, hist_ref[...])
        xa_ref[0, HALO:, :] = a_ref[...]
    else:
        for b in range(nb):
            xa_ref[b, HALO - CONV_HIST:HALO, :] = hist_ref[b]
            xa_ref[b, HALO:, :] = a_ref[b * seq_tile:(b + 1) * seq_tile, :]
    for b in range(nb):
        for r0 in range(0, seq_tile, CONV_ROWS):
            acc = jnp.broadcast_to(conv_b_ref[...], (CONV_ROWS, D_CONV))
            for j in range(CONV_WIDTH):
                start = HALO - CONV_HIST + r0 + j
                acc = acc + xa_ref[b, start:start + CONV_ROWS, :] * conv_w_ref[j:j + 1, :]
            y_ref[b * seq_tile + r0:b * seq_tile + r0 + CONV_ROWS, :] = acc
    y = y_ref[...]
    mu = jnp.mean(y, axis=-1, keepdims=True)
    yc = y - mu
    y = yc * lax.rsqrt(jnp.mean(yc * yc, axis=-1, keepdims=True) + EPS) * ln_g_ref[...] + ln_b_ref[...]
    conv_out = _dot((y * jax.nn.sigmoid(y)).astype(BF16), w_conv_out_ref[...])
    att_out = _dot(att_ref[...], w_attn_out_ref[...])

    x1 = x1_ref[...]
    u = _rms(x1, g_mix_ref[...]).astype(BF16)
    gates = jax.nn.sigmoid(_dot(u, w_gate_ref[...]))
    mix = gates[:, :D_MODEL] * conv_out + gates[:, D_MODEL:] * att_out
    x2 = x1 + _rms(_dot(mix.astype(BF16), w_out_ref[...]), g_mix_post_ref[...])

    x3 = _ffn(x2, g_pre_ref[...], g_post_ref[...], w2_ref, wd2_ref)

    gate = jax.nn.sigmoid(_dot(_rms(x3, g_ple_pre_ref[...]).astype(BF16), w_ple_gate_ref[...]))
    pe = _dot(p_ref[...].astype(BF16), w_ple_ref[...])
    x4 = x3 + _rms(pe * gate, g_ple_post_ref[...])
    o_ref[...] = _rms(x4, g_final_ref[...]) if final else x4


def _post_call(x1, a, hist, att, p, layer, w, seq_len, final):
    n = x1.shape[0]
    tile = TOKEN_TILE
    prompt = hist is None
    if prompt:
        assert seq_len % tile == 0 and tile % HALO == 0
        nb, seq_tile = 1, tile
        hist = a
        hist_spec = pl.BlockSpec((HALO, D_CONV), lambda i: (jnp.maximum(i * (tile // HALO) - 1, 0), 0))
    else:
        assert tile % seq_len == 0
        nb, seq_tile = tile // seq_len, seq_len
        hist_spec = pl.BlockSpec((None, nb, CONV_HIST, D_CONV), lambda i: (layer, i, 0, 0))
    params = (w['mix_pre'], w['conv_w'], w['conv_b'], w['conv_ln_g'], w['conv_ln_b'], w['w_conv_out'],
              w['w_attn_out'], w['w_gate'], w['w_out'], w['mix_post'], w['ffn2_pre'], w['ffn2_post'],
              w['ffn2_in'], w['ffn2_down'], w['ple_pre'], w['ple_post'], w['w_ple'], w['w_ple_gate'])
    final_spec = pl.BlockSpec(w['final_norm'].shape, lambda i: (0, 0), pipeline_mode=pl.Buffered(1))
    p_spec = pl.BlockSpec((None, tile, D_PLE), lambda i: (layer, i, 0))
    return pl.pallas_call(
        functools.partial(_post_kernel, seq_tile=seq_tile, seq_len=seq_len, prompt=prompt, final=final),
        grid=(n // tile,),
        in_specs=[_rows(D_MODEL), _rows(D_CONV), hist_spec, _rows(SB_WIDTH), p_spec]
                 + [_resident(q.shape, layer) for q in params] + [final_spec],
        out_specs=_rows(D_MODEL),
        out_shape=jax.ShapeDtypeStruct((n, D_MODEL), F32),
        scratch_shapes=[pltpu.VMEM((nb, HALO + seq_tile, D_CONV), F32), pltpu.VMEM((tile, D_CONV), F32)],
        compiler_params=pltpu.CompilerParams(dimension_semantics=("parallel",),
                                             vmem_limit_bytes=VMEM_LIMIT),
        name="post",
    )(x1, a, hist, att, p, *params, w['final_norm'])


def kernel(x_prompt, x_sample, p_prompt, p_sample, cache_k, cache_v, state_conv, ffn1_pre, ffn1_post, ffn1_in, ffn1_down, mix_pre, mix_post, w_in, conv_w, conv_b, conv_ln_g, conv_ln_b, w_conv_out, w_attn_out, w_out, ffn2_pre, ffn2_post, ffn2_in, ffn2_down, ple_pre, ple_post, w_ple, w_ple_gate, final_norm):
    batch, seq, _ = x_prompt.shape
    dec_batch, dec_seq, _ = x_sample.shape
    depth = w_in.shape[0]
    past = cache_k.shape[2]

    vec = lambda g: g.reshape(depth, 1, g.shape[-1])
    w = {
        'ffn1_pre': vec(ffn1_pre), 'ffn1_post': vec(ffn1_post), 'mix_pre': vec(mix_pre),
        'mix_post': vec(mix_post), 'ffn2_pre': vec(ffn2_pre), 'ffn2_post': vec(ffn2_post),
        'ple_pre': vec(ple_pre), 'ple_post': vec(ple_post), 'conv_b': vec(conv_b),
        'conv_ln_g': vec(conv_ln_g), 'conv_ln_b': vec(conv_ln_b), 'conv_w': conv_w,
        'final_norm': final_norm.reshape(1, D_MODEL),
        'ffn1_in': ffn1_in.astype(BF16), 'ffn1_down': ffn1_down.astype(BF16),
        'ffn2_in': ffn2_in.astype(BF16), 'ffn2_down': ffn2_down.astype(BF16),
        'w_proj': w_in[:, :, :D_PROJ].astype(BF16), 'w_gate': w_in[:, :, D_PROJ:].astype(BF16),
        'w_conv_out': w_conv_out.astype(BF16), 'w_attn_out': w_attn_out.astype(BF16),
        'w_out': w_out.astype(BF16), 'w_ple': w_ple.astype(BF16), 'w_ple_gate': w_ple_gate.astype(BF16),
    }
    tri = _suffix_matrix()
    xp = x_prompt.reshape(batch * seq, D_MODEL)
    xs = x_sample.reshape(dec_batch * dec_seq, D_MODEL)
    pp = p_prompt.reshape(depth, batch * seq, D_PLE)
    ps = p_sample.reshape(depth, dec_batch * dec_seq, D_PLE)
    ck = cache_k.reshape(depth, dec_batch, past, SB_WIDTH)
    cv = cache_v.reshape(depth, dec_batch, past, SB_WIDTH)

    outs = [[] for _ in range(6)]
    for i in range(depth):
        final = i == depth - 1
        x1p, ap, qp, kp, vp = _pre_call(xp, i, w)
        x1s, a_s, qs, ks, vs = _pre_call(xs, i, w)
        attp = _attn_prompt_call(qp, kp, vp, tri, batch, seq)
        atts = _attn_sample_call(qs, ks, vs, ck, cv, tri, i, dec_batch, dec_seq)
        xp = _post_call(x1p, ap, None, attp, pp, i, w, seq, final)
        xs = _post_call(x1s, a_s, state_conv, atts, ps, i, w, dec_seq, final)
        head_shape = (SB_HEADS, SB_HEAD_DIM)
        outs[0].append(kp.reshape(batch, seq, *head_shape))
        outs[1].append(vp.reshape(batch, seq, *head_shape))
        outs[2].append(ap.reshape(batch, seq, D_CONV)[:, seq - CONV_HIST:])
        outs[3].append(ks.reshape(dec_batch, dec_seq, *head_shape))
        outs[4].append(vs.reshape(dec_batch, dec_seq, *head_shape))
        new_state = jnp.concatenate([state_conv[i], a_s.reshape(dec_batch, dec_seq, D_CONV)], axis=1)
        outs[5].append(new_state[:, -CONV_HIST:])
    return (xp.reshape(batch, seq, D_MODEL), xs.reshape(dec_batch, dec_seq, D_MODEL),
            *[jnp.stack(o) for o in outs])
```

```python
import functools

import jax
import jax.numpy as jnp
from jax import lax
from jax.experimental import pallas as pl
from jax.experimental.pallas import tpu as pltpu

D_MODEL = 1024
DEPTH = 4
D_PLE = 256
D_CONV = D_MODEL // 2
CONV_WIDTH = 31
CONV_HIST = CONV_WIDTH - 1
SB_HEADS = 4
SB_HEAD_DIM = 128
SB_WIDTH = SB_HEADS * SB_HEAD_DIM
SB_SCALE = SB_HEAD_DIM ** -0.5
D_FF = 2816
EPS = 1e-6
D_PROJ = 2 * D_CONV + 3 * SB_WIDTH
D_GATE = 2 * D_MODEL

KEY_BLOCK = 128
HALO = 32
CONV_ROWS = 64
TOKEN_TILE = 256
VMEM_LIMIT = 60 * 1024 * 1024
UNDERFLOW_MARGIN = 90.0
NORM_SLACK = 1.05

F32 = jnp.float32
BF16 = jnp.bfloat16


def _rms(x, g):
    return x * lax.rsqrt(jnp.mean(x * x, axis=-1, keepdims=True) + EPS) * g


def _dot(a, b):
    return jnp.dot(a, b, preferred_element_type=F32)


def _ffn(x, g_pre, g_post, w_in_ref, w_down_ref):
    u = _rms(x, g_pre).astype(BF16)
    h = _dot(u, w_in_ref[...])
    gate, up = h[:, :D_FF], h[:, D_FF:]
    act = (gate * jax.nn.sigmoid(gate) * up).astype(BF16)
    y = _dot(act, w_down_ref[...])
    return x + 0.5 * _rms(y, g_post)


def _pre_kernel(x_ref, g_pre_ref, g_post_ref, w1_ref, wd_ref, g_mix_ref, w_proj_ref,
                x1_ref, a_ref, q_ref, k_ref, v_ref, kb_ref, vb_ref):
    x1 = _ffn(x_ref[...], g_pre_ref[...], g_post_ref[...], w1_ref, wd_ref)
    x1_ref[...] = x1
    u = _rms(x1, g_mix_ref[...]).astype(BF16)
    proj = _dot(u, w_proj_ref[...])
    c = D_CONV
    a_ref[...] = proj[:, :c] * jax.nn.sigmoid(proj[:, c:2 * c])
    c = 2 * c
    q_ref[...] = (proj[:, c:c + SB_WIDTH] * SB_SCALE).astype(BF16)
    k = proj[:, c + SB_WIDTH:c + 2 * SB_WIDTH]
    v = proj[:, c + 2 * SB_WIDTH:c + 3 * SB_WIDTH]
    k_ref[...] = k
    v_ref[...] = v
    kb_ref[...] = k.astype(BF16)
    vb_ref[...] = v.astype(BF16)


def _resident(shape, layer):
    nd = len(shape)
    return pl.BlockSpec((None,) + tuple(shape[1:]), lambda i: (layer,) + (0,) * (nd - 1),
                        pipeline_mode=pl.Buffered(1))


def _rows(width, tile=TOKEN_TILE):
    return pl.BlockSpec((tile, width), lambda i: (i, 0))


def _pre_call(x, layer, w):
    n = x.shape[0]
    params = (w['ffn1_pre'], w['ffn1_post'], w['ffn1_in'], w['ffn1_down'], w['mix_pre'], w['w_proj'])
    return pl.pallas_call(
        _pre_kernel,
        grid=(n // TOKEN_TILE,),
        in_specs=[_rows(D_MODEL)] + [_resident(p.shape, layer) for p in params],
        out_specs=[_rows(D_MODEL), _rows(D_CONV)] + [_rows(SB_WIDTH)] * 5,
        out_shape=[jax.ShapeDtypeStruct((n, D_MODEL), F32), jax.ShapeDtypeStruct((n, D_CONV), F32),
                   jax.ShapeDtypeStruct((n, SB_WIDTH), BF16), jax.ShapeDtypeStruct((n, SB_WIDTH), F32),
                   jax.ShapeDtypeStruct((n, SB_WIDTH), F32), jax.ShapeDtypeStruct((n, SB_WIDTH), BF16),
                   jax.ShapeDtypeStruct((n, SB_WIDTH), BF16)],
        compiler_params=pltpu.CompilerParams(dimension_semantics=("parallel",),
                                             vmem_limit_bytes=VMEM_LIMIT),
        name="pre",
    )(x, *params)


def _sb_chunk(q, k_rows, v_rows, tri, carry, acc, mask):
    z = lax.dot_general(q, k_rows, (((1,), (1,)), ((), ())), preferred_element_type=F32)
    sp = jnp.maximum(z, 0.0) + jnp.log(1.0 + jnp.exp(-jnp.abs(z)))
    if mask is not None:
        sp = jnp.where(mask, sp, 0.0)
    hi = sp.astype(BF16)
    lo = (sp - hi.astype(F32)).astype(BF16)
    log_w = [None] * (k_rows.shape[0] // KEY_BLOCK)
    for b in reversed(range(len(log_w))):
        cols = slice(b * KEY_BLOCK, (b + 1) * KEY_BLOCK)
        sums = _dot(jnp.concatenate([hi[:, cols], lo[:, cols]], axis=1), tri)
        log_w[b] = z[:, cols] - sums[:, :KEY_BLOCK] - carry
        carry = carry + sums[:, KEY_BLOCK:]
    w = jnp.exp(log_w[0] if len(log_w) == 1 else jnp.concatenate(log_w, axis=1))
    if mask is not None:
        w = jnp.where(mask, w, 0.0)
    return carry, acc + _dot(w.astype(BF16), v_rows)


def _row_sq_norms(x, ones):
    xf = x.astype(F32)
    return _dot((xf * xf).astype(BF16), ones)


def _sb_sweep(q_ref, kb_ref, vb_ref, tri, ones, kmax_ref, o_ref, last_row, last_mask, n_prev):
    heads = [slice(h * SB_HEAD_DIM, (h + 1) * SB_HEAD_DIM) for h in range(SB_HEADS)]
    qs = [q_ref[:, c] for c in heads]
    tq = qs[0].shape[0]
    z_bound = [jnp.sqrt(_row_sq_norms(q, ones) * (kmax_ref[h:h + 1, :] * NORM_SLACK))
               for h, q in enumerate(qs)]

    def slack(states):
        m = states[0][0] - z_bound[0]
        for (carry, _), zb in zip(states[1:], z_bound[1:]):
            m = jnp.minimum(m, carry - zb)
        return jnp.min(m)

    zeros = jnp.zeros((tq, KEY_BLOCK), F32)
    states = [_sb_chunk(q, kb_ref[pl.ds(last_row, KEY_BLOCK), c], vb_ref[pl.ds(last_row, KEY_BLOCK), c],
                        tri, zeros, zeros, last_mask) for q, c in zip(qs, heads)]

    def cond(loop):
        remaining, margin, _ = loop
        return jnp.logical_and(remaining > 0, margin <= UNDERFLOW_MARGIN)

    def body(loop):
        remaining, _, states = loop
        row = pl.multiple_of((remaining - 1) * KEY_BLOCK, KEY_BLOCK)
        rows = pl.ds(row, 2 * KEY_BLOCK)
        states = [_sb_chunk(q, kb_ref[rows, c], vb_ref[rows, c], tri, *st, None)
                  for q, c, st in zip(qs, heads, states)]
        return remaining - 2, slack(states), states

    _, _, states = lax.while_loop(cond, body, (jnp.int32(n_prev), slack(states), states))
    for c, (_, acc) in zip(heads, states):
        o_ref[:, c] = acc.astype(BF16)


def _stage_key_bounds(kb_ref, ones, kmax_ref):
    for h in range(SB_HEADS):
        n2 = _row_sq_norms(kb_ref[:, h * SB_HEAD_DIM:(h + 1) * SB_HEAD_DIM], ones)
        kmax_ref[h:h + 1, :] = jnp.max(n2, axis=0, keepdims=True)


def _suffix_matrix():
    j = lax.broadcasted_iota(jnp.int32, (KEY_BLOCK, KEY_BLOCK), 0)
    s = lax.broadcasted_iota(jnp.int32, (KEY_BLOCK, KEY_BLOCK), 1)
    half = jnp.concatenate([(j >= s).astype(BF16), jnp.ones((KEY_BLOCK, KEY_BLOCK), BF16)], axis=1)
    return jnp.concatenate([half, half], axis=0)


def _attn_prompt_kernel(q_ref, k_ref, v_ref, tri_ref, ones_ref, o_ref, kb_ref, vb_ref, kmax_ref):
    i = pl.program_id(1)
    ones = ones_ref[...]

    @pl.when(i == 0)
    def _():
        zeros = jnp.zeros((KEY_BLOCK, SB_WIDTH), BF16)
        kb_ref[0:KEY_BLOCK, :] = zeros
        vb_ref[0:KEY_BLOCK, :] = zeros
        kb_ref[KEY_BLOCK:, :] = k_ref[...]
        vb_ref[KEY_BLOCK:, :] = v_ref[...]
        _stage_key_bounds(kb_ref, ones, kmax_ref)

    t = lax.broadcasted_iota(jnp.int32, (KEY_BLOCK, KEY_BLOCK), 0)
    s = lax.broadcasted_iota(jnp.int32, (KEY_BLOCK, KEY_BLOCK), 1)
    last_row = pl.multiple_of((i + 1) * KEY_BLOCK, KEY_BLOCK)
    _sb_sweep(q_ref, kb_ref, vb_ref, tri_ref[...], ones, kmax_ref, o_ref, last_row, s < t, i)


def _const_spec(x):
    return pl.BlockSpec(x.shape, lambda *_: (0,) * x.ndim)


def _attn_prompt_call(q, kb, vb, tri, ones, batch, seq):
    n = q.shape[0]
    nq = seq // KEY_BLOCK
    q_spec = pl.BlockSpec((KEY_BLOCK, SB_WIDTH), lambda b, i: (b * nq + i, 0))
    kv_spec = pl.BlockSpec((seq, SB_WIDTH), lambda b, i: (b, 0), pipeline_mode=pl.Buffered(1))
    staged = pltpu.VMEM((KEY_BLOCK + seq, SB_WIDTH), BF16)
    return pl.pallas_call(
        _attn_prompt_kernel,
        grid=(batch, nq),
        in_specs=[q_spec, kv_spec, kv_spec, _const_spec(tri), _const_spec(ones)],
        out_specs=q_spec,
        out_shape=jax.ShapeDtypeStruct((n, SB_WIDTH), BF16),
        scratch_shapes=[staged, staged, pltpu.VMEM((8, KEY_BLOCK), F32)],
        compiler_params=pltpu.CompilerParams(
            dimension_semantics=("parallel", "arbitrary"), vmem_limit_bytes=VMEM_LIMIT),
        name="attn_prompt",
    )(q, kb, vb, tri, ones)


def _attn_sample_kernel(q_ref, k_ref, v_ref, ck_ref, cv_ref, tri_ref, ones_ref, o_ref,
                        kb_ref, vb_ref, kmax_ref, *, past, t_new):
    ones = ones_ref[...]
    new_row = KEY_BLOCK + past
    for new, cache, dst in ((k_ref, ck_ref, kb_ref), (v_ref, cv_ref, vb_ref)):
        dst[0:KEY_BLOCK, :] = jnp.zeros((KEY_BLOCK, SB_WIDTH), BF16)
        dst[KEY_BLOCK:new_row, :] = cache[...].astype(BF16)
        dst[new_row:new_row + t_new, :] = new[...]
        dst[new_row + t_new:, :] = jnp.zeros((KEY_BLOCK - t_new, SB_WIDTH), BF16)
    _stage_key_bounds(kb_ref, ones, kmax_ref)
    t = lax.broadcasted_iota(jnp.int32, (t_new, KEY_BLOCK), 0)
    s = lax.broadcasted_iota(jnp.int32, (t_new, KEY_BLOCK), 1)
    _sb_sweep(q_ref, kb_ref, vb_ref, tri_ref[...], ones, kmax_ref, o_ref, new_row, s < t,
              past // KEY_BLOCK)


def _attn_sample_call(q, kb, vb, cache_k, cache_v, tri, ones, layer, batch, t_new):
    n = q.shape[0]
    past = cache_k.shape[2]
    assert past % KEY_BLOCK == 0 and t_new < KEY_BLOCK
    row_spec = pl.BlockSpec((t_new, SB_WIDTH), lambda b: (b, 0))
    cache_spec = pl.BlockSpec((None, None, past, SB_WIDTH), lambda b: (layer, b, 0, 0))
    staged = pltpu.VMEM((past + 2 * KEY_BLOCK, SB_WIDTH), BF16)
    return pl.pallas_call(
        functools.partial(_attn_sample_kernel, past=past, t_new=t_new),
        grid=(batch,),
        in_specs=[row_spec, row_spec, row_spec, cache_spec, cache_spec, _const_spec(tri), _const_spec(ones)],
        out_specs=row_spec,
        out_shape=jax.ShapeDtypeStruct((n, SB_WIDTH), BF16),
        scratch_shapes=[staged, staged, pltpu.VMEM((8, KEY_BLOCK), F32)],
        compiler_params=pltpu.CompilerParams(dimension_semantics=("parallel",),
                                             vmem_limit_bytes=VMEM_LIMIT),
        name="attn_sample",
    )(q, kb, vb, cache_k, cache_v, tri, ones)


def _post_kernel(x1_ref, a_ref, hist_ref, att_ref, p_ref,
                 g_mix_ref, conv_w_ref, conv_b_ref, ln_g_ref, ln_b_ref, w_conv_out_ref, w_attn_out_ref,
                 w_gate_ref, w_out_ref, g_mix_post_ref, g_pre_ref, g_post_ref, w2_ref, wd2_ref,
                 g_ple_pre_ref, g_ple_post_ref, w_ple_ref, w_ple_gate_ref, g_final_ref,
                 o_ref, xa_ref, y_ref, *, seq_tile, seq_len, prompt, final):
    nb = xa_ref.shape[0]
    if prompt:
        at_start = (pl.program_id(0) * seq_tile) % seq_len == 0
        xa_ref[0, 0:HALO, :] = jnp.where(at_start, 0.0, hist_ref[...])
        xa_ref[0, HALO:, :] = a_ref[...]
    else:
        for b in range(nb):
            xa_ref[b, HALO - CONV_HIST:HALO, :] = hist_ref[b]
            xa_ref[b, HALO:, :] = a_ref[b * seq_tile:(b + 1) * seq_tile, :]
    for b in range(nb):
        for r0 in range(0, seq_tile, CONV_ROWS):
            acc = jnp.broadcast_to(conv_b_ref[...], (CONV_ROWS, D_CONV))
            for j in range(CONV_WIDTH):
                start = HALO - CONV_HIST + r0 + j
                acc = acc + xa_ref[b, start:start + CONV_ROWS, :] * conv_w_ref[j:j + 1, :]
            y_ref[b * seq_tile + r0:b * seq_tile + r0 + CONV_ROWS, :] = acc
    y = y_ref[...]
    mu = jnp.mean(y, axis=-1, keepdims=True)
    yc = y - mu
    y = yc * lax.rsqrt(jnp.mean(yc * yc, axis=-1, keepdims=True) + EPS) * ln_g_ref[...] + ln_b_ref[...]
    conv_out = _dot((y * jax.nn.sigmoid(y)).astype(BF16), w_conv_out_ref[...])
    att_out = _dot(att_ref[...], w_attn_out_ref[...])

    x1 = x1_ref[...]
    u = _rms(x1, g_mix_ref[...]).astype(BF16)
    gates = jax.nn.sigmoid(_dot(u, w_gate_ref[...]))
    mix = gates[:, :D_MODEL] * conv_out + gates[:, D_MODEL:] * att_out
    x2 = x1 + _rms(_dot(mix.astype(BF16), w_out_ref[...]), g_mix_post_ref[...])

    x3 = _ffn(x2, g_pre_ref[...], g_post_ref[...], w2_ref, wd2_ref)

    gate = jax.nn.sigmoid(_dot(_rms(x3, g_ple_pre_ref[...]).astype(BF16), w_ple_gate_ref[...]))
    pe = _dot(p_ref[...].astype(BF16), w_ple_ref[...])
    x4 = x3 + _rms(pe * gate, g_ple_post_ref[...])
    o_ref[...] = _rms(x4, g_final_ref[...]) if final else x4


def _post_call(x1, a, hist, att, p, layer, w, seq_len, final):
    n = x1.shape[0]
    tile = TOKEN_TILE
    prompt = hist is None
    if prompt:
        assert seq_len % tile == 0 and tile % HALO == 0
        nb, seq_tile = 1, tile
        hist = a
        hist_spec = pl.BlockSpec((HALO, D_CONV), lambda i: (jnp.maximum(i * (tile // HALO) - 1, 0), 0))
    else:
        assert tile % seq_len == 0
        nb, seq_tile = tile // seq_len, seq_len
        hist_spec = pl.BlockSpec((None, nb, CONV_HIST, D_CONV), lambda i: (layer, i, 0, 0))
    params = (w['mix_pre'], w['conv_w'], w['conv_b'], w['conv_ln_g'], w['conv_ln_b'], w['w_conv_out'],
              w['w_attn_out'], w['w_gate'], w['w_out'], w['mix_post'], w['ffn2_pre'], w['ffn2_post'],
              w['ffn2_in'], w['ffn2_down'], w['ple_pre'], w['ple_post'], w['w_ple'], w['w_ple_gate'])
    final_spec = pl.BlockSpec(w['final_norm'].shape, lambda i: (0, 0), pipeline_mode=pl.Buffered(1))
    p_spec = pl.BlockSpec((None, tile, D_PLE), lambda i: (layer, i, 0))
    return pl.pallas_call(
        functools.partial(_post_kernel, seq_tile=seq_tile, seq_len=seq_len, prompt=prompt, final=final),
        grid=(n // tile,),
        in_specs=[_rows(D_MODEL), _rows(D_CONV), hist_spec, _rows(SB_WIDTH), p_spec]
                 + [_resident(q.shape, layer) for q in params] + [final_spec],
        out_specs=_rows(D_MODEL),
        out_shape=jax.ShapeDtypeStruct((n, D_MODEL), F32),
        scratch_shapes=[pltpu.VMEM((nb, HALO + seq_tile, D_CONV), F32), pltpu.VMEM((tile, D_CONV), F32)],
        compiler_params=pltpu.CompilerParams(dimension_semantics=("parallel",),
                                             vmem_limit_bytes=VMEM_LIMIT),
        name="post",
    )(x1, a, hist, att, p, *params, w['final_norm'])


def kernel(x_prompt, x_sample, p_prompt, p_sample, cache_k, cache_v, state_conv, ffn1_pre, ffn1_post, ffn1_in, ffn1_down, mix_pre, mix_post, w_in, conv_w, conv_b, conv_ln_g, conv_ln_b, w_conv_out, w_attn_out, w_out, ffn2_pre, ffn2_post, ffn2_in, ffn2_down, ple_pre, ple_post, w_ple, w_ple_gate, final_norm):
    batch, seq, _ = x_prompt.shape
    dec_batch, dec_seq, _ = x_sample.shape
    depth = w_in.shape[0]
    past = cache_k.shape[2]

    vec = lambda g: g.reshape(depth, 1, g.shape[-1])
    w = {
        'ffn1_pre': vec(ffn1_pre), 'ffn1_post': vec(ffn1_post), 'mix_pre': vec(mix_pre),
        'mix_post': vec(mix_post), 'ffn2_pre': vec(ffn2_pre), 'ffn2_post': vec(ffn2_post),
        'ple_pre': vec(ple_pre), 'ple_post': vec(ple_post), 'conv_b': vec(conv_b),
        'conv_ln_g': vec(conv_ln_g), 'conv_ln_b': vec(conv_ln_b), 'conv_w': conv_w,
        'final_norm': final_norm.reshape(1, D_MODEL),
        'ffn1_in': ffn1_in.astype(BF16), 'ffn1_down': ffn1_down.astype(BF16),
        'ffn2_in': ffn2_in.astype(BF16), 'ffn2_down': ffn2_down.astype(BF16),
        'w_proj': w_in[:, :, :D_PROJ].astype(BF16), 'w_gate': w_in[:, :, D_PROJ:].astype(BF16),
        'w_conv_out': w_conv_out.astype(BF16), 'w_attn_out': w_attn_out.astype(BF16),
        'w_out': w_out.astype(BF16), 'w_ple': w_ple.astype(BF16), 'w_ple_gate': w_ple_gate.astype(BF16),
    }
    tri = _suffix_matrix()
    ones = jnp.ones((SB_HEAD_DIM, KEY_BLOCK), BF16)
    xp =x_prompt.reshape(batch * seq, D_MODEL)
    xs = x_sample.reshape(dec_batch * dec_seq, D_MODEL)
    pp = p_prompt.reshape(depth, batch * seq, D_PLE)
    ps = p_sample.reshape(depth, dec_batch * dec_seq, D_PLE)
    ck = cache_k.reshape(depth, dec_batch, past, SB_WIDTH)
    cv = cache_v.reshape(depth, dec_batch, past, SB_WIDTH)

    outs = [[] for _ in range(6)]
    for i in range(depth):
        final = i == depth - 1
        x1p, ap, qp, kp, vp, kbp, vbp = _pre_call(xp, i, w)
        x1s, a_s, qs, ks, vs, kbs, vbs = _pre_call(xs, i, w)
        attp = _attn_prompt_call(qp, kbp, vbp, tri, ones, batch, seq)
        atts = _attn_sample_call(qs, kbs, vbs, ck, cv, tri, ones, i, dec_batch, dec_seq)
        xp = _post_call(x1p, ap, None, attp, pp, i, w, seq, final)
        xs = _post_call(x1s, a_s, state_conv, atts, ps, i, w, dec_seq, final)
        head_shape = (SB_HEADS, SB_HEAD_DIM)
        outs[0].append(kp.reshape(batch, seq, *head_shape))
        outs[1].append(vp.reshape(batch, seq, *head_shape))
        outs[2].append(ap.reshape(batch, seq, D_CONV)[:, seq - CONV_HIST:])
        outs[3].append(ks.reshape(dec_batch, dec_seq, *head_shape))
        outs[4].append(vs.reshape(dec_batch, dec_seq, *head_shape))
        new_state = jnp.concatenate([state_conv[i], a_s.reshape(dec_batch, dec_seq, D_CONV)], axis=1)
        outs[5].append(new_state[:, -CONV_HIST:])
    return (xp.reshape(batch, seq, D_MODEL), xs.reshape(dec_batch, dec_seq, D_MODEL),
            *[jnp.stack(o) for o in outs])
```

```python
import functools

import jax
import jax.numpy as jnp
from jax import lax
from jax.experimental import pallas as pl
from jax.experimental.pallas import tpu as pltpu

D_MODEL = 1024
DEPTH = 4
D_PLE = 256
D_CONV = D_MODEL // 2
CONV_WIDTH = 31
CONV_HIST = CONV_WIDTH - 1
SB_HEADS = 4
SB_HEAD_DIM = 128
SB_WIDTH = SB_HEADS * SB_HEAD_DIM
SB_SCALE = SB_HEAD_DIM ** -0.5
D_FF = 2816
EPS = 1e-6
D_PROJ = 2 * D_CONV + 3 * SB_WIDTH
D_GATE = 2 * D_MODEL

KEY_BLOCK = 128
LEAD_BLOCKS = 2
LEAD_ROWS = LEAD_BLOCKS * KEY_BLOCK
QUERY_TILES = 2
HALO = 32
CONV_ROWS = 64
SUBLANES = 8
TOKEN_TILE = 256
VMEM_LIMIT = 60 * 1024 * 1024
UNDERFLOW_MARGIN = 90.0
NORM_SLACK = 1.05

F32 = jnp.float32
BF16 = jnp.bfloat16


def _rms(x, g):
    return x * lax.rsqrt(jnp.mean(x * x, axis=-1, keepdims=True) + EPS) * g


def _dot(a, b):
    return jnp.dot(a, b, preferred_element_type=F32)


def _ffn(x, g_pre, g_post, w_in_ref, w_down_ref):
    u = _rms(x, g_pre).astype(BF16)
    h = _dot(u, w_in_ref[...])
    gate, up = h[:, :D_FF], h[:, D_FF:]
    act = (gate * jax.nn.sigmoid(gate) * up).astype(BF16)
    y = _dot(act, w_down_ref[...])
    return x + 0.5 * _rms(y, g_post)


def _issue_after(x, anchor):
    bits = lax.bitcast_convert_type(anchor, jnp.uint32)
    zero = lax.shift_right_logical(lax.shift_right_logical(bits, jnp.uint32(16)), jnp.uint32(16))
    return lax.bitcast_convert_type(lax.bitcast_convert_type(x, jnp.uint32) | zero, F32)


def _conv_shifted_copies(xa_ref, xs_ref, anchor):
    rows = xs_ref.shape[2]
    for b in range(xa_ref.shape[0]):
        for r in range(1, SUBLANES):
            xs_ref[b, r - 1] = _issue_after(xa_ref[b, r:r + rows, :], anchor)


def _conv_rows(b, r0, conv_w_ref, conv_b_ref, xa_ref, xs_ref, anchor):
    acc = _issue_after(jnp.broadcast_to(conv_b_ref[...], (CONV_ROWS, D_CONV)), anchor)
    for j in range(CONV_WIDTH):
        shift = (HALO - CONV_HIST + j) % SUBLANES
        start = r0 + HALO - CONV_HIST + j - shift
        src = xa_ref[b, start:start + CONV_ROWS, :] if shift == 0 else xs_ref[b, shift - 1, start:start + CONV_ROWS, :]
        acc = acc + src * conv_w_ref[j:j + 1, :]
    return acc


def _pre_kernel(*refs, prompt, aliased, seq_tile, seq_len, n_tiles):
    x_ref, g_pre_ref, g_post_ref, w1_ref, wd_ref, g_mix_ref, w_proj_ref, conv_w_ref, conv_b_ref = refs[:9]
    rest = refs[9:]
    if not prompt:
        hist_ref, rest = rest[0], rest[1:]
    if aliased:
        rest = rest[2:]
    x1_ref, a_ref, q_ref, k_ref, v_ref, kb_ref, vb_ref, y_ref, xa_ref, xs_ref = rest
    i = pl.program_id(0)
    nb = xa_ref.shape[0]

    @pl.when(i == 0)
    def _():
        xa_ref[...] = jnp.zeros(xa_ref.shape, F32)

    x = x_ref[...]
    un = _rms(x, g_pre_ref[...])
    anchor = un[0:1, :D_CONV]
    _conv_shifted_copies(xa_ref, xs_ref, anchor)
    for b in range(nb):
        for r0 in range(0, seq_tile, CONV_ROWS):
            row = b * seq_tile + r0
            y_ref[row:row + CONV_ROWS, :] = _conv_rows(b, r0, conv_w_ref, conv_b_ref, xa_ref, xs_ref, anchor)

    h = _dot(un.astype(BF16), w1_ref[...])
    act = (h[:, :D_FF] * jax.nn.sigmoid(h[:, :D_FF]) * h[:, D_FF:]).astype(BF16)
    x1 = x + 0.5 * _rms(_dot(act, wd_ref[...]), g_post_ref[...])
    x1_ref[...] = x1
    u = _rms(x1, g_mix_ref[...]).astype(BF16)
    proj = _dot(u, w_proj_ref[...])
    c = D_CONV
    a = proj[:, :c] * jax.nn.sigmoid(proj[:, c:2 * c])
    a_ref[...] = a
    c = 2 * c
    q_ref[...] = (proj[:, c:c + SB_WIDTH] * SB_SCALE).astype(BF16)
    k = proj[:, c + SB_WIDTH:c + 2 * SB_WIDTH]
    v = proj[:, c + 2 * SB_WIDTH:c + 3 * SB_WIDTH]
    rows = x_ref.shape[0]
    for hd in range(SB_HEADS):
        cols = slice(hd * SB_HEAD_DIM, (hd + 1) * SB_HEAD_DIM)
        k_ref[pl.ds(hd, rows, stride=SB_HEADS), :] = k[:, cols]
        v_ref[pl.ds(hd, rows, stride=SB_HEADS), :] = v[:, cols]
    kb_ref[...] = k.astype(BF16)
    vb_ref[...] = v.astype(BF16)

    if prompt:
        at_start = (jnp.minimum(i, n_tiles - 1) * seq_tile) % seq_len == 0
        xa_ref[0, 0:HALO, :] = jnp.where(at_start, 0.0, xa_ref[0, seq_tile:seq_tile + HALO, :])
        xa_ref[0, HALO:, :] = a
    else:
        for b in range(nb):
            xa_ref[b, 0:HALO - CONV_HIST, :] = jnp.zeros((HALO - CONV_HIST, D_CONV), F32)
            xa_ref[b, HALO - CONV_HIST:HALO, :] = hist_ref[b]
            xa_ref[b, HALO:, :] = a[b * seq_tile:(b + 1) * seq_tile, :]


def _resident(shape, layer):
    nd = len(shape)
    return pl.BlockSpec((None,) + tuple(shape[1:]), lambda i: (layer,) + (0,) * (nd - 1),
                        pipeline_mode=pl.Buffered(1))


def _rows(width, tile=TOKEN_TILE):
    return pl.BlockSpec((tile, width), lambda i: (i, 0))


def _pre_call(x, hist, layer, w, kv_results, seq_len):
    n = x.shape[0]
    tile = TOKEN_TILE
    n_tiles = n // tile
    depth = w['w_proj'].shape[0]
    prompt = hist is None
    cur = lambda i: jnp.minimum(i, n_tiles - 1)
    rows_cur = lambda width: pl.BlockSpec((tile, width), lambda i: (cur(i), 0))
    params = (w['ffn1_pre'], w['ffn1_post'], w['ffn1_in'], w['ffn1_down'], w['mix_pre'], w['w_proj'],
              w['conv_w'], w['conv_b'])
    inputs = (x,) + params
    in_specs = [rows_cur(D_MODEL)] + [_resident(p.shape, layer) for p in params]
    if prompt:
        assert seq_len % tile == 0
        nb, seq_tile = 1, tile
    else:
        assert tile % seq_len == 0
        nb, seq_tile = tile // seq_len, seq_len
        inputs += (hist,)
        in_specs += [pl.BlockSpec((None, nb, CONV_HIST, D_CONV), lambda i: (layer, cur(i), 0, 0))]
    aliases = {}
    if kv_results is not None:
        aliases = {len(inputs): 3, len(inputs) + 1: 4}
        inputs += tuple(kv_results)
        in_specs += [pl.BlockSpec(memory_space=pl.ANY)] * 2
    kv_spec = pl.BlockSpec((None, SB_HEADS * tile, SB_HEAD_DIM), lambda i: (layer, cur(i), 0))
    kv_shape = jax.ShapeDtypeStruct((depth, SB_HEADS * n, SB_HEAD_DIM), F32)
    conv_spec = pl.BlockSpec((tile, D_CONV), lambda i: (jnp.maximum(i - 1, 0), 0))
    return pl.pallas_call(
        functools.partial(_pre_kernel, prompt=prompt, aliased=kv_results is not None, seq_tile=seq_tile,
                          seq_len=seq_len, n_tiles=n_tiles),
        grid=(n_tiles + 1,),
        in_specs=in_specs,
        out_specs=[rows_cur(D_MODEL), rows_cur(D_CONV), rows_cur(SB_WIDTH), kv_spec, kv_spec,
                   rows_cur(SB_WIDTH), rows_cur(SB_WIDTH), conv_spec],
        out_shape=[jax.ShapeDtypeStruct((n, D_MODEL), F32), jax.ShapeDtypeStruct((n, D_CONV), F32),
                   jax.ShapeDtypeStruct((n, SB_WIDTH), BF16), kv_shape, kv_shape,
                   jax.ShapeDtypeStruct((n, SB_WIDTH), BF16), jax.ShapeDtypeStruct((n, SB_WIDTH), BF16),
                   jax.ShapeDtypeStruct((n, D_CONV), F32)],
        scratch_shapes=[pltpu.VMEM((nb, HALO + seq_tile, D_CONV), F32),
                        pltpu.VMEM((nb, SUBLANES - 1, HALO + seq_tile - SUBLANES, D_CONV), F32)],
        input_output_aliases=aliases,
        compiler_params=pltpu.CompilerParams(dimension_semantics=("arbitrary",),
                                             vmem_limit_bytes=VMEM_LIMIT),
        name="pre",
    )(*inputs)


def _sb_chunk(q, k_rows, v_rows, tri, carry, acc, mask):
    tq = q.shape[0]
    n = k_rows.shape[0] // KEY_BLOCK
    blocks = [slice(b * KEY_BLOCK, (b + 1) * KEY_BLOCK) for b in range(n)]
    valid = [None] * n if mask is None else mask
    z = lax.dot_general(q, k_rows, (((1,), (1,)), ((), ())), preferred_element_type=F32)
    sp = jnp.maximum(z, 0.0) + jnp.log(1.0 + jnp.exp(-jnp.abs(z)))
    sp = [sp[:, c] if m is None else jnp.where(m, sp[:, c], 0.0) for m, c in zip(valid, blocks)]
    hi = [s.astype(BF16) for s in sp]
    lo = [(s - h.astype(F32)).astype(BF16) for s, h in zip(sp, hi)]
    sums = _dot(jnp.concatenate([jnp.concatenate([h, l], axis=1) for h, l in zip(hi, lo)], axis=0), tri)
    w = [None] * n
    for b in reversed(range(n)):
        sums_b = sums[b * tq:(b + 1) * tq]
        w_b = jnp.exp(z[:, blocks[b]] - sums_b[:, :KEY_BLOCK] - carry)
        w[b] = (w_b if valid[b] is None else jnp.where(valid[b], w_b, 0.0)).astype(BF16)
        carry = carry + sums_b[:, KEY_BLOCK:]
    w = w[0] if n == 1 else jnp.concatenate(w, axis=1)
    return carry, acc + _dot(w, v_rows)


def _block_row(b):
    return b * KEY_BLOCK if isinstance(b, int) else pl.multiple_of(b * KEY_BLOCK, KEY_BLOCK)


def _row_sq_norms(x, ones):
    xf = x.astype(F32)
    return _dot((xf * xf).astype(BF16), ones)


def _sb_sweep(q_ref, kb_ref, vb_ref, tri, ones, kmax_ref, o_ref, first_block, first_mask):
    heads = [slice(h * SB_HEAD_DIM, (h + 1) * SB_HEAD_DIM) for h in range(SB_HEADS)]
    qs = [q_ref[:, c] for c in heads]
    tq = qs[0].shape[0]
    z_bound = [jnp.sqrt(_row_sq_norms(q, ones) * (kmax_ref[h:h + 1, :] * NORM_SLACK))
               for h, q in enumerate(qs)]

    def slack(states):
        m = states[0][0] - z_bound[0]
        for (carry, _), zb in zip(states[1:], z_bound[1:]):
            m = jnp.minimum(m, carry - zb)
        return jnp.min(m)

    def advance(states, block, n_blocks, mask):
        rows = pl.ds(_block_row(block), n_blocks * KEY_BLOCK)
        return [_sb_chunk(q, kb_ref[rows, c], vb_ref[rows, c], tri, *st, mask)
                for q, c, st in zip(qs, heads, states)]

    zeros = jnp.zeros((tq, KEY_BLOCK), F32)
    first_n = len(first_mask)
    states = advance([(zeros, zeros)] * SB_HEADS, first_block, first_n, first_mask)

    def cond(loop):
        remaining, margin, _ = loop
        return jnp.logical_and(remaining > 0, margin <= UNDERFLOW_MARGIN)

    def body(loop):
        remaining, _, states = loop
        states = advance(states, remaining + LEAD_BLOCKS - 2, 2, None)
        return remaining - 2, slack(states), states

    remaining = jnp.int32(first_block - LEAD_BLOCKS)
    _, _, states = lax.while_loop(cond, body, (remaining, slack(states), states))
    for c, (_, acc) in zip(heads, states):
        o_ref[:, c] = acc.astype(BF16)


def _first_run_mask(tq, n_blocks):
    t = lax.broadcasted_iota(jnp.int32, (tq, KEY_BLOCK), 0)
    s = lax.broadcasted_iota(jnp.int32, (tq, KEY_BLOCK), 1)
    return [None if b < LEAD_BLOCKS else s + (b - LEAD_BLOCKS) * KEY_BLOCK < t for b in range(n_blocks)]


def _stage_key_bounds(kb_ref, ones, kmax_ref):
    for h in range(SB_HEADS):
        n2 = _row_sq_norms(kb_ref[:, h * SB_HEAD_DIM:(h + 1) * SB_HEAD_DIM], ones)
        kmax_ref[h:h + 1, :] = jnp.max(n2, axis=0, keepdims=True)


def _suffix_matrix():
    j = lax.broadcasted_iota(jnp.int32, (KEY_BLOCK, KEY_BLOCK), 0)
    s = lax.broadcasted_iota(jnp.int32, (KEY_BLOCK, KEY_BLOCK), 1)
    half = jnp.concatenate([(j >= s).astype(BF16), jnp.ones((KEY_BLOCK, KEY_BLOCK), BF16)], axis=1)
    return jnp.concatenate([half, half], axis=0)


def _attn_prompt_kernel(q_ref, k_ref, v_ref, tri_ref, ones_ref, o_ref, kb_ref, vb_ref, kmax_ref):
    i = pl.program_id(1)
    ones = ones_ref[...]

    @pl.when(i == 0)
    def _():
        zeros = jnp.zeros((LEAD_ROWS, SB_WIDTH), BF16)
        kb_ref[0:LEAD_ROWS, :] = zeros
        vb_ref[0:LEAD_ROWS, :] = zeros
        kb_ref[LEAD_ROWS:, :] = k_ref[...]
        vb_ref[LEAD_ROWS:, :] = v_ref[...]
        _stage_key_bounds(kb_ref, ones, kmax_ref)

    own = q_ref.shape[0] // KEY_BLOCK
    _sb_sweep(q_ref, kb_ref, vb_ref, tri_ref[...], ones, kmax_ref, o_ref, i * own,
              _first_run_mask(q_ref.shape[0], LEAD_BLOCKS + own))


def _const_spec(x):
    return pl.BlockSpec(x.shape, lambda *_: (0,) * x.ndim)


def _attn_prompt_call(q, kb, vb, tri, ones, batch, seq):
    n = q.shape[0]
    rows = QUERY_TILES * KEY_BLOCK
    assert seq % rows == 0
    nq = seq // rows
    q_spec = pl.BlockSpec((rows, SB_WIDTH), lambda b, i: (b * nq + i, 0))
    kv_spec = pl.BlockSpec((seq, SB_WIDTH), lambda b, i: (b, 0), pipeline_mode=pl.Buffered(1))
    staged = pltpu.VMEM((LEAD_ROWS + seq, SB_WIDTH), BF16)
    return pl.pallas_call(
        _attn_prompt_kernel,
        grid=(batch, nq),
        in_specs=[q_spec, kv_spec, kv_spec, _const_spec(tri), _const_spec(ones)],
        out_specs=q_spec,
        out_shape=jax.ShapeDtypeStruct((n, SB_WIDTH), BF16),
        scratch_shapes=[staged, staged, pltpu.VMEM((8, KEY_BLOCK), F32)],
        compiler_params=pltpu.CompilerParams(
            dimension_semantics=("parallel", "arbitrary"), vmem_limit_bytes=VMEM_LIMIT),
        name="attn_prompt",
    )(q, kb, vb, tri, ones)


def _attn_sample_kernel(q_ref, k_ref, v_ref, ck_ref, cv_ref, tri_ref, ones_ref, o_ref,
                        kb_ref, vb_ref, kmax_ref, *, past, t_new):
    ones = ones_ref[...]
    new_row = LEAD_ROWS + past
    for new, cache, dst in ((k_ref, ck_ref, kb_ref), (v_ref, cv_ref, vb_ref)):
        dst[0:LEAD_ROWS, :] = jnp.zeros((LEAD_ROWS, SB_WIDTH), BF16)
        for h in range(SB_HEADS):
            dst[LEAD_ROWS:new_row, h * SB_HEAD_DIM:(h + 1) * SB_HEAD_DIM] = (
                cache[pl.ds(h, past, stride=SB_HEADS), :].astype(BF16))
        dst[new_row:new_row + t_new, :] = new[...]
        dst[new_row + t_new:, :] = jnp.zeros((KEY_BLOCK - t_new, SB_WIDTH), BF16)
    _stage_key_bounds(kb_ref, ones, kmax_ref)
    _sb_sweep(q_ref, kb_ref, vb_ref, tri_ref[...], ones, kmax_ref, o_ref, past // KEY_BLOCK,
              _first_run_mask(t_new, LEAD_BLOCKS + 1))


def _attn_sample_call(q, kb, vb, cache_k, cache_v, tri, ones, layer, batch, t_new):
    n = q.shape[0]
    past = cache_k.shape[2] // SB_HEADS
    assert past % KEY_BLOCK == 0 and t_new < KEY_BLOCK
    row_spec = pl.BlockSpec((t_new, SB_WIDTH), lambda b: (b, 0))
    cache_spec = pl.BlockSpec((None, None, past * SB_HEADS, SB_HEAD_DIM), lambda b: (layer, b, 0, 0))
    staged = pltpu.VMEM((LEAD_ROWS + past + KEY_BLOCK, SB_WIDTH), BF16)
    return pl.pallas_call(
        functools.partial(_attn_sample_kernel, past=past, t_new=t_new),
        grid=(batch,),
        in_specs=[row_spec, row_spec, row_spec, cache_spec, cache_spec, _const_spec(tri), _const_spec(ones)],
        out_specs=row_spec,
        out_shape=jax.ShapeDtypeStruct((n, SB_WIDTH), BF16),
        scratch_shapes=[staged, staged, pltpu.VMEM((8, KEY_BLOCK), F32)],
        compiler_params=pltpu.CompilerParams(dimension_semantics=("parallel",),
                                             vmem_limit_bytes=VMEM_LIMIT),
        name="attn_sample",
    )(q, kb, vb, cache_k, cache_v, tri, ones)


def _post_kernel(x1_ref, y_ref, att_ref, p_ref,
                 g_mix_ref, ln_g_ref, ln_b_ref, w_conv_out_ref, w_attn_out_ref,
                 w_gate_ref, w_out_ref, g_mix_post_ref, g_pre_ref, g_post_ref, w2_ref, wd2_ref,
                 g_ple_pre_ref, g_ple_post_ref, w_ple_ref, w_ple_gate_ref, g_final_ref,
                 o_ref, *, final):
    y = y_ref[...]
    mu = jnp.mean(y, axis=-1, keepdims=True)
    yc = y - mu
    y = yc * lax.rsqrt(jnp.mean(yc * yc, axis=-1, keepdims=True) + EPS) * ln_g_ref[...] + ln_b_ref[...]
    conv_out = _dot((y * jax.nn.sigmoid(y)).astype(BF16), w_conv_out_ref[...])
    att_out = _dot(att_ref[...], w_attn_out_ref[...])

    x1 = x1_ref[...]
    u = _rms(x1, g_mix_ref[...]).astype(BF16)
    gates = jax.nn.sigmoid(_dot(u, w_gate_ref[...]))
    mix = gates[:, :D_MODEL] * conv_out + gates[:, D_MODEL:] * att_out
    x2 = x1 + _rms(_dot(mix.astype(BF16), w_out_ref[...]), g_mix_post_ref[...])

    x3 = _ffn(x2, g_pre_ref[...], g_post_ref[...], w2_ref, wd2_ref)

    gate = jax.nn.sigmoid(_dot(_rms(x3, g_ple_pre_ref[...]).astype(BF16), w_ple_gate_ref[...]))
    pe = _dot(p_ref[...].astype(BF16), w_ple_ref[...])
    x4 = x3 + _rms(pe * gate, g_ple_post_ref[...])
    o_ref[...] = _rms(x4, g_final_ref[...]) if final else x4


def _post_call(x1, y_conv, att, p, layer, w, final):
    n = x1.shape[0]
    tile = TOKEN_TILE
    params = (w['mix_pre'], w['conv_ln_g'], w['conv_ln_b'], w['w_conv_out'],
              w['w_attn_out'], w['w_gate'], w['w_out'], w['mix_post'], w['ffn2_pre'], w['ffn2_post'],
              w['ffn2_in'], w['ffn2_down'], w['ple_pre'], w['ple_post'], w['w_ple'], w['w_ple_gate'])
    final_spec = pl.BlockSpec(w['final_norm'].shape, lambda i: (0, 0), pipeline_mode=pl.Buffered(1))
    p_spec = pl.BlockSpec((None, tile, D_PLE), lambda i: (layer, i, 0))
    return pl.pallas_call(
        functools.partial(_post_kernel, final=final),
        grid=(n // tile,),
        in_specs=[_rows(D_MODEL), _rows(D_CONV), _rows(SB_WIDTH), p_spec]
                 + [_resident(q.shape, layer) for q in params] + [final_spec],
        out_specs=_rows(D_MODEL),
        out_shape=jax.ShapeDtypeStruct((n, D_MODEL), F32),
        compiler_params=pltpu.CompilerParams(dimension_semantics=("parallel",),
                                             vmem_limit_bytes=VMEM_LIMIT),
        name="post",
    )(x1, y_conv, att, p, *params, w['final_norm'])


def kernel(x_prompt, x_sample, p_prompt, p_sample, cache_k, cache_v, state_conv, ffn1_pre, ffn1_post, ffn1_in, ffn1_down, mix_pre, mix_post, w_in, conv_w, conv_b, conv_ln_g, conv_ln_b, w_conv_out, w_attn_out, w_out, ffn2_pre, ffn2_post, ffn2_in, ffn2_down, ple_pre, ple_post, w_ple, w_ple_gate, final_norm):
    batch, seq, _ = x_prompt.shape
    dec_batch, dec_seq, _ = x_sample.shape
    depth = w_in.shape[0]
    past = cache_k.shape[2]

    vec = lambda g: g.reshape(depth, 1, g.shape[-1])
    w = {
        'ffn1_pre': vec(ffn1_pre), 'ffn1_post': vec(ffn1_post), 'mix_pre': vec(mix_pre),
        'mix_post': vec(mix_post), 'ffn2_pre': vec(ffn2_pre), 'ffn2_post': vec(ffn2_post),
        'ple_pre': vec(ple_pre), 'ple_post': vec(ple_post), 'conv_b': vec(conv_b),
        'conv_ln_g': vec(conv_ln_g), 'conv_ln_b': vec(conv_ln_b), 'conv_w': conv_w,
        'final_norm': final_norm.reshape(1, D_MODEL),
        'ffn1_in': ffn1_in.astype(BF16), 'ffn1_down': ffn1_down.astype(BF16),
        'ffn2_in': ffn2_in.astype(BF16), 'ffn2_down': ffn2_down.astype(BF16),
        'w_proj': w_in[:, :, :D_PROJ].astype(BF16), 'w_gate': w_in[:, :, D_PROJ:].astype(BF16),
        'w_conv_out': w_conv_out.astype(BF16), 'w_attn_out': w_attn_out.astype(BF16),
        'w_out': w_out.astype(BF16), 'w_ple': w_ple.astype(BF16), 'w_ple_gate': w_ple_gate.astype(BF16),
    }
    tri = _suffix_matrix()
    ones = jnp.ones((SB_HEAD_DIM, KEY_BLOCK), BF16)
    xp = x_prompt.reshape(batch * seq, D_MODEL)
    xs = x_sample.reshape(dec_batch * dec_seq, D_MODEL)
    pp = p_prompt.reshape(depth, batch * seq, D_PLE)
    ps = p_sample.reshape(depth, dec_batch * dec_seq, D_PLE)
    ck = cache_k.reshape(depth, dec_batch, past * SB_HEADS, SB_HEAD_DIM)
    cv = cache_v.reshape(depth, dec_batch, past * SB_HEADS, SB_HEAD_DIM)

    kv_p = kv_s = None
    conv_p, conv_s = [], []
    for i in range(depth):
        final = i == depth - 1
        x1p, ap, qp, *kv_p, kbp, vbp, yp = _pre_call(xp, None, i, w, kv_p, seq)
        x1s, a_s, qs, *kv_s, kbs, vbs, ys = _pre_call(xs, state_conv, i, w, kv_s, dec_seq)
        attp = _attn_prompt_call(qp, kbp, vbp, tri, ones, batch, seq)
        atts = _attn_sample_call(qs, kbs, vbs, ck, cv, tri, ones, i, dec_batch, dec_seq)
        xp = _post_call(x1p, yp, attp, pp, i, w, final)
        xs = _post_call(x1s, ys, atts, ps, i, w, final)
        conv_p.append(ap.reshape(batch, seq, D_CONV)[:, seq - CONV_HIST:])
        new_state = jnp.concatenate([state_conv[i], a_s.reshape(dec_batch, dec_seq, D_CONV)], axis=1)
        conv_s.append(new_state[:, -CONV_HIST:])
    kv_shape_p = (depth, batch, seq, SB_HEADS, SB_HEAD_DIM)
    kv_shape_s = (depth, dec_batch, dec_seq, SB_HEADS, SB_HEAD_DIM)
    return (xp.reshape(batch, seq, D_MODEL), xs.reshape(dec_batch, dec_seq, D_MODEL),
            kv_p[0].reshape(kv_shape_p), kv_p[1].reshape(kv_shape_p), jnp.stack(conv_p),
            kv_s[0].reshape(kv_shape_s), kv_s[1].reshape(kv_shape_s), jnp.stack(conv_s))
```

```python
import functools

import jax
import jax.numpy as jnp
from jax import lax
from jax.experimental import pallas as pl
from jax.experimental.pallas import tpu as pltpu

D_MODEL = 1024
DEPTH = 4
D_PLE = 256
D_CONV = D_MODEL // 2
CONV_WIDTH = 31
CONV_HIST = CONV_WIDTH - 1
SB_HEADS = 4
SB_HEAD_DIM = 128
SB_WIDTH = SB_HEADS * SB_HEAD_DIM
SB_SCALE = SB_HEAD_DIM ** -0.5
D_FF = 2816
EPS = 1e-6
D_PROJ = 2 * D_CONV + 3 * SB_WIDTH
D_GATE = 2 * D_MODEL

KEY_BLOCK = 128
LEAD_BLOCKS = 2
LEAD_ROWS = LEAD_BLOCKS * KEY_BLOCK
QUERY_TILES = 2
HALO = 32
CONV_ROWS = 64
SUBLANES = 8
TOKEN_TILE = 512
VMEM_LIMIT = 60 * 1024 * 1024
UNDERFLOW_MARGIN = 90.0
NORM_SLACK = 1.05

F32 = jnp.float32
BF16 = jnp.bfloat16


def _rms(x, g):
    return x * lax.rsqrt(jnp.mean(x * x, axis=-1, keepdims=True) + EPS) * g


def _dot(a, b):
    return jnp.dot(a, b, preferred_element_type=F32)


def _ffn(x, g_pre, g_post, w_in_ref, w_down_ref):
    u = _rms(x, g_pre).astype(BF16)
    h = _dot(u, w_in_ref[...])
    gate, up = h[:, :D_FF], h[:, D_FF:]
    act = (gate * jax.nn.sigmoid(gate) * up).astype(BF16)
    y = _dot(act, w_down_ref[...])
    return x + 0.5 * _rms(y, g_post)


def _issue_after(x, anchor):
    bits = lax.bitcast_convert_type(anchor, jnp.uint32)
    zero = lax.shift_right_logical(lax.shift_right_logical(bits, jnp.uint32(16)), jnp.uint32(16))
    return lax.bitcast_convert_type(lax.bitcast_convert_type(x, jnp.uint32) | zero, F32)


def _conv_shifted_copies(xa_ref, xs_ref, anchor):
    rows = xs_ref.shape[2]
    for b in range(xa_ref.shape[0]):
        for r in range(1, SUBLANES):
            xs_ref[b, r - 1] = _issue_after(xa_ref[b, r:r + rows, :], anchor)


def _conv_rows(b, r0, conv_w_ref, conv_b_ref, xa_ref, xs_ref, anchor):
    acc = _issue_after(jnp.broadcast_to(conv_b_ref[...], (CONV_ROWS, D_CONV)), anchor)
    for j in range(CONV_WIDTH):
        shift = (HALO - CONV_HIST + j) % SUBLANES
        start = r0 + HALO - CONV_HIST + j - shift
        src = xa_ref[b, start:start + CONV_ROWS, :] if shift == 0 else xs_ref[b, shift - 1, start:start + CONV_ROWS, :]
        acc = acc + src * conv_w_ref[j:j + 1, :]
    return acc


def _pre_kernel(*refs, prompt, aliased, seq_tile, seq_len, n_tiles):
    x_ref, g_pre_ref, g_post_ref, w1_ref, wd_ref, g_mix_ref, w_proj_ref, conv_w_ref, conv_b_ref = refs[:9]
    rest = refs[9:]
    if not prompt:
        hist_ref, rest = rest[0], rest[1:]
    if aliased:
        rest = rest[2:]
    x1_ref, a_ref, q_ref, k_ref, v_ref, kb_ref, vb_ref, y_ref, xa_ref, xs_ref = rest
    i = pl.program_id(0)
    nb = xa_ref.shape[0]

    @pl.when(i == 0)
    def _():
        xa_ref[...] = jnp.zeros(xa_ref.shape, F32)

    x = x_ref[...]
    un = _rms(x, g_pre_ref[...])
    anchor = un[0:1, :D_CONV]
    _conv_shifted_copies(xa_ref, xs_ref, anchor)
    for b in range(nb):
        for r0 in range(0, seq_tile, CONV_ROWS):
            row = b * seq_tile + r0
            y_ref[row:row + CONV_ROWS, :] = _conv_rows(b, r0, conv_w_ref, conv_b_ref, xa_ref, xs_ref, anchor)

    h = _dot(un.astype(BF16), w1_ref[...])
    act = (h[:, :D_FF] * jax.nn.sigmoid(h[:, :D_FF]) * h[:, D_FF:]).astype(BF16)
    x1 = x + 0.5 * _rms(_dot(act, wd_ref[...]), g_post_ref[...])
    x1_ref[...] = x1
    u = _rms(x1, g_mix_ref[...]).astype(BF16)
    proj = _dot(u, w_proj_ref[...])
    c = D_CONV
    a = proj[:, :c] * jax.nn.sigmoid(proj[:, c:2 * c])
    a_ref[...] = a
    c = 2 * c
    q_ref[...] = (proj[:, c:c + SB_WIDTH] * SB_SCALE).astype(BF16)
    k = proj[:, c + SB_WIDTH:c + 2 * SB_WIDTH]
    v = proj[:, c + 2 * SB_WIDTH:c + 3 * SB_WIDTH]
    rows = x_ref.shape[0]
    for hd in range(SB_HEADS):
        cols = slice(hd * SB_HEAD_DIM, (hd + 1) * SB_HEAD_DIM)
        k_ref[pl.ds(hd, rows, stride=SB_HEADS), :] = k[:, cols]
        v_ref[pl.ds(hd, rows, stride=SB_HEADS), :] = v[:, cols]
    kb_ref[...] = k.astype(BF16)
    vb_ref[...] = v.astype(BF16)

    if prompt:
        at_start = (jnp.minimum(i, n_tiles - 1) * seq_tile) % seq_len == 0
        xa_ref[0, 0:HALO, :] = jnp.where(at_start, 0.0, xa_ref[0, seq_tile:seq_tile + HALO, :])
        xa_ref[0, HALO:, :] = a
    else:
        for b in range(nb):
            xa_ref[b, 0:HALO - CONV_HIST, :] = jnp.zeros((HALO - CONV_HIST, D_CONV), F32)
            xa_ref[b, HALO - CONV_HIST:HALO, :] = hist_ref[b]
            xa_ref[b, HALO:, :] = a[b * seq_tile:(b + 1) * seq_tile, :]


def _resident(shape, layer):
    nd = len(shape)
    return pl.BlockSpec((None,) + tuple(shape[1:]), lambda i: (layer,) + (0,) * (nd - 1),
                        pipeline_mode=pl.Buffered(1))


def _rows(width, tile=TOKEN_TILE):
    return pl.BlockSpec((tile, width), lambda i: (i, 0))


def _pre_call(x, hist, layer, w, kv_results, seq_len):
    n = x.shape[0]
    tile = TOKEN_TILE
    n_tiles = n // tile
    depth = w['w_proj'].shape[0]
    prompt = hist is None
    cur = lambda i: jnp.minimum(i, n_tiles - 1)
    rows_cur = lambda width: pl.BlockSpec((tile, width), lambda i: (cur(i), 0))
    params = (w['ffn1_pre'], w['ffn1_post'], w['ffn1_in'], w['ffn1_down'], w['mix_pre'], w['w_proj'],
              w['conv_w'], w['conv_b'])
    inputs = (x,) + params
    in_specs = [rows_cur(D_MODEL)] + [_resident(p.shape, layer) for p in params]
    if prompt:
        assert seq_len % tile == 0
        nb, seq_tile = 1, tile
    else:
        assert tile % seq_len == 0
        nb, seq_tile = tile // seq_len, seq_len
        inputs += (hist,)
        in_specs += [pl.BlockSpec((None, nb, CONV_HIST, D_CONV), lambda i: (layer, cur(i), 0, 0))]
    aliases = {}
    if kv_results is not None:
        aliases = {len(inputs): 3, len(inputs) + 1: 4}
        inputs += tuple(kv_results)
        in_specs += [pl.BlockSpec(memory_space=pl.ANY)] * 2
    kv_spec = pl.BlockSpec((None, SB_HEADS * tile, SB_HEAD_DIM), lambda i: (layer, cur(i), 0))
    kv_shape = jax.ShapeDtypeStruct((depth, SB_HEADS * n, SB_HEAD_DIM), F32)
    conv_spec = pl.BlockSpec((tile, D_CONV), lambda i: (jnp.maximum(i - 1, 0), 0))
    return pl.pallas_call(
        functools.partial(_pre_kernel, prompt=prompt, aliased=kv_results is not None, seq_tile=seq_tile,
                          seq_len=seq_len, n_tiles=n_tiles),
        grid=(n_tiles + 1,),
        in_specs=in_specs,
        out_specs=[rows_cur(D_MODEL), rows_cur(D_CONV), rows_cur(SB_WIDTH), kv_spec, kv_spec,
                   rows_cur(SB_WIDTH), rows_cur(SB_WIDTH), conv_spec],
        out_shape=[jax.ShapeDtypeStruct((n, D_MODEL), F32), jax.ShapeDtypeStruct((n, D_CONV), F32),
                   jax.ShapeDtypeStruct((n, SB_WIDTH), BF16), kv_shape, kv_shape,
                   jax.ShapeDtypeStruct((n, SB_WIDTH), BF16), jax.ShapeDtypeStruct((n, SB_WIDTH), BF16),
                   jax.ShapeDtypeStruct((n, D_CONV), F32)],
        scratch_shapes=[pltpu.VMEM((nb, HALO + seq_tile, D_CONV), F32),
                        pltpu.VMEM((nb, SUBLANES - 1, HALO + seq_tile - SUBLANES, D_CONV), F32)],
        input_output_aliases=aliases,
        compiler_params=pltpu.CompilerParams(dimension_semantics=("arbitrary",),
                                             vmem_limit_bytes=VMEM_LIMIT),
        name="pre",
    )(*inputs)


def _sb_chunk(q, k_rows, v_rows, tri, carry, acc, mask):
    tq = q.shape[0]
    n = k_rows.shape[0] // KEY_BLOCK
    blocks = [slice(b * KEY_BLOCK, (b + 1) * KEY_BLOCK) for b in range(n)]
    valid = [None] * n if mask is None else mask
    z = lax.dot_general(q, k_rows, (((1,), (1,)), ((), ())), preferred_element_type=F32)
    sp = jnp.maximum(z, 0.0) + jnp.log(1.0 + jnp.exp(-jnp.abs(z)))
    sp = [sp[:, c] if m is None else jnp.where(m, sp[:, c], 0.0) for m, c in zip(valid, blocks)]
    hi = [s.astype(BF16) for s in sp]
    lo = [(s - h.astype(F32)).astype(BF16) for s, h in zip(sp, hi)]
    sums = _dot(jnp.concatenate([jnp.concatenate([h, l], axis=1) for h, l in zip(hi, lo)], axis=0), tri)
    w = [None] * n
    for b in reversed(range(n)):
        sums_b = sums[b * tq:(b + 1) * tq]
        w_b = jnp.exp(z[:, blocks[b]] - sums_b[:, :KEY_BLOCK] - carry)
        w[b] = (w_b if valid[b] is None else jnp.where(valid[b], w_b, 0.0)).astype(BF16)
        carry = carry + sums_b[:, KEY_BLOCK:]
    w = w[0] if n == 1 else jnp.concatenate(w, axis=1)
    return carry, acc + _dot(w, v_rows)


def _block_row(b):
    return b * KEY_BLOCK if isinstance(b, int) else pl.multiple_of(b * KEY_BLOCK, KEY_BLOCK)


def _row_sq_norms(x, ones):
    xf = x.astype(F32)
    return _dot((xf * xf).astype(BF16), ones)


def _sb_sweep(q_ref, kb_ref, vb_ref, tri, ones, kmax_ref, o_ref, first_block, first_mask):
    heads = [slice(h * SB_HEAD_DIM, (h + 1) * SB_HEAD_DIM) for h in range(SB_HEADS)]
    qs = [q_ref[:, c] for c in heads]
    tq = qs[0].shape[0]
    z_bound = [jnp.sqrt(_row_sq_norms(q, ones) * (kmax_ref[h:h + 1, :] * NORM_SLACK))
               for h, q in enumerate(qs)]

    def slack(states):
        m = states[0][0] - z_bound[0]
        for (carry, _), zb in zip(states[1:], z_bound[1:]):
            m = jnp.minimum(m, carry - zb)
        return jnp.min(m)

    def advance(states, block, n_blocks, mask):
        rows = pl.ds(_block_row(block), n_blocks * KEY_BLOCK)
        return [_sb_chunk(q, kb_ref[rows, c], vb_ref[rows, c], tri, *st, mask)
                for q, c, st in zip(qs, heads, states)]

    zeros = jnp.zeros((tq, KEY_BLOCK), F32)
    first_n = len(first_mask)
    states = advance([(zeros, zeros)] * SB_HEADS, first_block, first_n, first_mask)

    def cond(loop):
        remaining, margin, _ = loop
        return jnp.logical_and(remaining > 0, margin <= UNDERFLOW_MARGIN)

    def body(loop):
        remaining, _, states = loop
        states = advance(states, remaining + LEAD_BLOCKS - 2, 2, None)
        return remaining - 2, slack(states), states

    remaining = jnp.int32(first_block - LEAD_BLOCKS)
    _, _, states = lax.while_loop(cond, body, (remaining, slack(states), states))
    for c, (_, acc) in zip(heads, states):
        o_ref[:, c] = acc.astype(BF16)


def _first_run_mask(tq, n_blocks):
    t = lax.broadcasted_iota(jnp.int32, (tq, KEY_BLOCK), 0)
    s = lax.broadcasted_iota(jnp.int32, (tq, KEY_BLOCK), 1)
    return [None if b < LEAD_BLOCKS else s + (b - LEAD_BLOCKS) * KEY_BLOCK < t for b in range(n_blocks)]


def _stage_key_bounds(kb_ref, ones, kmax_ref):
    for h in range(SB_HEADS):
        n2 = _row_sq_norms(kb_ref[:, h * SB_HEAD_DIM:(h + 1) * SB_HEAD_DIM], ones)
        kmax_ref[h:h + 1, :] = jnp.max(n2, axis=0, keepdims=True)


def _suffix_matrix():
    j = lax.broadcasted_iota(jnp.int32, (KEY_BLOCK, KEY_BLOCK), 0)
    s = lax.broadcasted_iota(jnp.int32, (KEY_BLOCK, KEY_BLOCK), 1)
    half = jnp.concatenate([(j >= s).astype(BF16), jnp.ones((KEY_BLOCK, KEY_BLOCK), BF16)], axis=1)
    return jnp.concatenate([half, half], axis=0)


def _attn_prompt_kernel(q_ref, k_ref, v_ref, tri_ref, ones_ref, o_ref, kb_ref, vb_ref, kmax_ref):
    i = pl.program_id(1)
    ones = ones_ref[...]

    @pl.when(i == 0)
    def _():
        zeros = jnp.zeros((LEAD_ROWS, SB_WIDTH), BF16)
        kb_ref[0:LEAD_ROWS, :] = zeros
        vb_ref[0:LEAD_ROWS, :] = zeros
        kb_ref[LEAD_ROWS:, :] = k_ref[...]
        vb_ref[LEAD_ROWS:, :] = v_ref[...]
        _stage_key_bounds(kb_ref, ones, kmax_ref)

    own = q_ref.shape[0] // KEY_BLOCK
    _sb_sweep(q_ref, kb_ref, vb_ref, tri_ref[...], ones, kmax_ref, o_ref, i * own,
              _first_run_mask(q_ref.shape[0], LEAD_BLOCKS + own))


def _const_spec(x):
    return pl.BlockSpec(x.shape, lambda *_: (0,) * x.ndim)


def _attn_prompt_call(q, kb, vb, tri, ones, batch, seq):
    n = q.shape[0]
    rows = QUERY_TILES * KEY_BLOCK
    assert seq % rows == 0
    nq = seq // rows
    q_spec = pl.BlockSpec((rows, SB_WIDTH), lambda b, i: (b * nq + i, 0))
    kv_spec = pl.BlockSpec((seq, SB_WIDTH), lambda b, i: (b, 0), pipeline_mode=pl.Buffered(1))
    staged = pltpu.VMEM((LEAD_ROWS + seq, SB_WIDTH), BF16)
    return pl.pallas_call(
        _attn_prompt_kernel,
        grid=(batch, nq),
        in_specs=[q_spec, kv_spec, kv_spec, _const_spec(tri), _const_spec(ones)],
        out_specs=q_spec,
        out_shape=jax.ShapeDtypeStruct((n, SB_WIDTH), BF16),
        scratch_shapes=[staged, staged, pltpu.VMEM((8, KEY_BLOCK), F32)],
        compiler_params=pltpu.CompilerParams(
            dimension_semantics=("parallel", "arbitrary"), vmem_limit_bytes=VMEM_LIMIT),
        name="attn_prompt",
    )(q, kb, vb, tri, ones)


def _attn_sample_kernel(q_ref, k_ref, v_ref, ck_ref, cv_ref, tri_ref, ones_ref, o_ref,
                        kb_ref, vb_ref, kmax_ref, *, past, t_new):
    ones = ones_ref[...]
    new_row = LEAD_ROWS + past
    for new, cache, dst in ((k_ref, ck_ref, kb_ref), (v_ref, cv_ref, vb_ref)):
        dst[0:LEAD_ROWS, :] = jnp.zeros((LEAD_ROWS, SB_WIDTH), BF16)
        for h in range(SB_HEADS):
            dst[LEAD_ROWS:new_row, h * SB_HEAD_DIM:(h + 1) * SB_HEAD_DIM] = (
                cache[pl.ds(h, past, stride=SB_HEADS), :].astype(BF16))
        dst[new_row:new_row + t_new, :] = new[...]
        dst[new_row + t_new:, :] = jnp.zeros((KEY_BLOCK - t_new, SB_WIDTH), BF16)
    _stage_key_bounds(kb_ref, ones, kmax_ref)
    _sb_sweep(q_ref, kb_ref, vb_ref, tri_ref[...], ones, kmax_ref, o_ref, past // KEY_BLOCK,
              _first_run_mask(t_new, LEAD_BLOCKS + 1))


def _attn_sample_call(q, kb, vb, cache_k, cache_v, tri, ones, layer, batch, t_new):
    n = q.shape[0]
    past = cache_k.shape[2] // SB_HEADS
    assert past % KEY_BLOCK == 0 and t_new < KEY_BLOCK
    row_spec = pl.BlockSpec((t_new, SB_WIDTH), lambda b: (b, 0))
    cache_spec = pl.BlockSpec((None, None, past * SB_HEADS, SB_HEAD_DIM), lambda b: (layer, b, 0, 0))
    staged = pltpu.VMEM((LEAD_ROWS + past + KEY_BLOCK, SB_WIDTH), BF16)
    return pl.pallas_call(
        functools.partial(_attn_sample_kernel, past=past, t_new=t_new),
        grid=(batch,),
        in_specs=[row_spec, row_spec, row_spec, cache_spec, cache_spec, _const_spec(tri), _const_spec(ones)],
        out_specs=row_spec,
        out_shape=jax.ShapeDtypeStruct((n, SB_WIDTH), BF16),
        scratch_shapes=[staged, staged, pltpu.VMEM((8, KEY_BLOCK), F32)],
        compiler_params=pltpu.CompilerParams(dimension_semantics=("parallel",),
                                             vmem_limit_bytes=VMEM_LIMIT),
        name="attn_sample",
    )(q, kb, vb, cache_k, cache_v, tri, ones)


def _post_kernel(x1_ref, y_ref, att_ref, p_ref,
                 g_mix_ref, ln_g_ref, ln_b_ref, w_conv_out_ref, w_attn_out_ref,
                 w_gate_ref, w_out_ref, g_mix_post_ref, g_pre_ref, g_post_ref, w2_ref, wd2_ref,
                 g_ple_pre_ref, g_ple_post_ref, w_ple_ref, w_ple_gate_ref, g_final_ref,
                 o_ref, *, final):
    y = y_ref[...]
    mu = jnp.mean(y, axis=-1, keepdims=True)
    yc = y - mu
    y = yc * lax.rsqrt(jnp.mean(yc * yc, axis=-1, keepdims=True) + EPS) * ln_g_ref[...] + ln_b_ref[...]
    conv_out = _dot((y * jax.nn.sigmoid(y)).astype(BF16), w_conv_out_ref[...])
    att_out = _dot(att_ref[...], w_attn_out_ref[...])

    x1 = x1_ref[...]
    u = _rms(x1, g_mix_ref[...]).astype(BF16)
    gates = jax.nn.sigmoid(_dot(u, w_gate_ref[...]))
    mix = gates[:, :D_MODEL] * conv_out + gates[:, D_MODEL:] * att_out
    x2 = x1 + _rms(_dot(mix.astype(BF16), w_out_ref[...]), g_mix_post_ref[...])

    x3 = _ffn(x2, g_pre_ref[...], g_post_ref[...], w2_ref, wd2_ref)

    gate = jax.nn.sigmoid(_dot(_rms(x3, g_ple_pre_ref[...]).astype(BF16), w_ple_gate_ref[...]))
    pe = _dot(p_ref[...].astype(BF16), w_ple_ref[...])
    x4 = x3 + _rms(pe * gate, g_ple_post_ref[...])
    o_ref[...] = _rms(x4, g_final_ref[...]) if final else x4


def _post_call(x1, y_conv, att, p, layer, w, final):
    n = x1.shape[0]
    tile = TOKEN_TILE
    params = (w['mix_pre'], w['conv_ln_g'], w['conv_ln_b'], w['w_conv_out'],
              w['w_attn_out'], w['w_gate'], w['w_out'], w['mix_post'], w['ffn2_pre'], w['ffn2_post'],
              w['ffn2_in'], w['ffn2_down'], w['ple_pre'], w['ple_post'], w['w_ple'], w['w_ple_gate'])
    final_spec = pl.BlockSpec(w['final_norm'].shape, lambda i: (0, 0), pipeline_mode=pl.Buffered(1))
    p_spec = pl.BlockSpec((None, tile, D_PLE), lambda i: (layer, i, 0))
    return pl.pallas_call(
        functools.partial(_post_kernel, final=final),
        grid=(n // tile,),
        in_specs=[_rows(D_MODEL), _rows(D_CONV), _rows(SB_WIDTH), p_spec]
                 + [_resident(q.shape, layer) for q in params] + [final_spec],
        out_specs=_rows(D_MODEL),
        out_shape=jax.ShapeDtypeStruct((n, D_MODEL), F32),
        compiler_params=pltpu.CompilerParams(dimension_semantics=("parallel",),
                                             vmem_limit_bytes=VMEM_LIMIT),
        name="post",
    )(x1, y_conv, att, p, *params, w['final_norm'])


def kernel(x_prompt, x_sample, p_prompt, p_sample, cache_k, cache_v, state_conv, ffn1_pre, ffn1_post, ffn1_in, ffn1_down, mix_pre, mix_post, w_in, conv_w, conv_b, conv_ln_g, conv_ln_b, w_conv_out, w_attn_out, w_out, ffn2_pre, ffn2_post, ffn2_in, ffn2_down, ple_pre, ple_post, w_ple, w_ple_gate, final_norm):
    batch, seq, _ = x_prompt.shape
    dec_batch, dec_seq, _ = x_sample.shape
    depth = w_in.shape[0]
    past = cache_k.shape[2]

    vec = lambda g: g.reshape(depth, 1, g.shape[-1])
    w = {
        'ffn1_pre': vec(ffn1_pre), 'ffn1_post': vec(ffn1_post), 'mix_pre': vec(mix_pre),
        'mix_post': vec(mix_post), 'ffn2_pre': vec(ffn2_pre), 'ffn2_post': vec(ffn2_post),
        'ple_pre': vec(ple_pre), 'ple_post': vec(ple_post), 'conv_b': vec(conv_b),
        'conv_ln_g': vec(conv_ln_g), 'conv_ln_b': vec(conv_ln_b), 'conv_w': conv_w,
        'final_norm': final_norm.reshape(1, D_MODEL),
        'ffn1_in': ffn1_in.astype(BF16), 'ffn1_down': ffn1_down.astype(BF16),
        'ffn2_in': ffn2_in.astype(BF16), 'ffn2_down': ffn2_down.astype(BF16),
        'w_proj': w_in[:, :, :D_PROJ].astype(BF16), 'w_gate': w_in[:, :, D_PROJ:].astype(BF16),
        'w_conv_out': w_conv_out.astype(BF16), 'w_attn_out': w_attn_out.astype(BF16),
        'w_out': w_out.astype(BF16), 'w_ple': w_ple.astype(BF16), 'w_ple_gate': w_ple_gate.astype(BF16),
    }
    tri = _suffix_matrix()
    ones = jnp.ones((SB_HEAD_DIM, KEY_BLOCK), BF16)
    xp = x_prompt.reshape(batch * seq, D_MODEL)
    xs = x_sample.reshape(dec_batch * dec_seq, D_MODEL)
    pp = p_prompt.reshape(depth, batch * seq, D_PLE)
    ps = p_sample.reshape(depth, dec_batch * dec_seq, D_PLE)
    ck = cache_k.reshape(depth, dec_batch, past * SB_HEADS, SB_HEAD_DIM)
    cv = cache_v.reshape(depth, dec_batch, past * SB_HEADS, SB_HEAD_DIM)

    kv_p = kv_s = None
    conv_p, conv_s = [], []
    for i in range(depth):
        final = i == depth - 1
        x1p, ap, qp, *kv_p, kbp, vbp, yp = _pre_call(xp, None, i, w, kv_p, seq)
        x1s, a_s, qs, *kv_s, kbs, vbs, ys = _pre_call(xs, state_conv, i, w, kv_s, dec_seq)
        attp = _attn_prompt_call(qp, kbp, vbp, tri, ones, batch, seq)
        atts = _attn_sample_call(qs, kbs, vbs, ck, cv, tri, ones, i, dec_batch, dec_seq)
        xp = _post_call(x1p, yp, attp, pp, i, w, final)
        xs = _post_call(x1s, ys, atts, ps, i, w, final)
        conv_p.append(ap.reshape(batch, seq, D_CONV)[:, seq - CONV_HIST:])
        new_state = jnp.concatenate([state_conv[i], a_s.reshape(dec_batch, dec_seq, D_CONV)], axis=1)
        conv_s.append(new_state[:, -CONV_HIST:])
    kv_shape_p = (depth, batch, seq, SB_HEADS, SB_HEAD_DIM)
    kv_shape_s = (depth, dec_batch, dec_seq, SB_HEADS, SB_HEAD_DIM)
    return (xp.reshape(batch, seq, D_MODEL), xs.reshape(dec_batch, dec_seq, D_MODEL),
            kv_p[0].reshape(kv_shape_p), kv_p[1].reshape(kv_shape_p), jnp.stack(conv_p),
            kv_s[0].reshape(kv_shape_s), kv_s[1].reshape(kv_shape_s), jnp.stack(conv_s))
```
